```python
import jax, jax.numpy as jnp
from jax import lax
import numpy as np

D_MODEL = 1024
BATCH = 4
SEQ = 8192
DEPTH = 2

N_MIXERS = 2
ATTN_GROUPS = ((128, 1), (512, 4), (2048, 16))
N_ATTN_GROUPS = len(ATTN_GROUPS)
HEADS_PER_GROUP = 16
HEAD_DIM = D_MODEL // HEADS_PER_GROUP
ATTN_WIDTH = HEADS_PER_GROUP * HEAD_DIM
ATTN_QKV_COLS = N_ATTN_GROUPS * 3 * ATTN_WIDTH
SGU_CHUNK = 128
SGU_WIDTH = 2 * D_MODEL
SGU_GROUP_CH = 128
SGU_GROUPS = SGU_WIDTH // SGU_GROUP_CH
DEEPNORM_ALPHA = (2 * DEPTH) ** 0.25
DEEPNORM_BETA = (8 * DEPTH) ** -0.25
LN_EPS = 1e-5
NEG_INF = -1e30
N_A_LAYERS = (DEPTH + 1) // 2
N_B_LAYERS = DEPTH // 2

kernel_name = "hybrid_dilated_swa_sgu_encoder"


def layer_norm(x, g, b):
    xf = x.astype(jnp.float32)
    mu = jnp.mean(xf, axis=-1, keepdims=True)
    var = jnp.mean(jnp.square(xf - mu), axis=-1, keepdims=True)
    return ((xf - mu) * lax.rsqrt(var + LN_EPS) * g.astype(jnp.float32)
            + b.astype(jnp.float32)).astype(x.dtype)


def alibi_slopes(n):
    return 2.0 ** (-8.0 * jnp.arange(1, n + 1, dtype=jnp.float32) / n)


def dilated_window_attention(q, k, v, slopes, window, dilation):
    B, S, H, Dh = q.shape
    d = dilation
    L = S // d
    w = window // (2 * d)
    nb = -(-L // w)
    Lp = nb * w

    def strided(a):
        return a.reshape(B, L, d, H, Dh).transpose(0, 2, 3, 1, 4)

    qb = jnp.pad(strided(q), ((0, 0),) * 3 + ((0, Lp - L), (0, 0))).reshape(B, d, H, nb, w, Dh)
    pad_kv = ((0, 0),) * 3 + ((w, Lp - L + w), (0, 0))

    def band(a):
        ab = jnp.pad(strided(a), pad_kv).reshape(B, d, H, nb + 2, w, Dh)
        return jnp.concatenate([ab[:, :, :, :-2], ab[:, :, :, 1:-1], ab[:, :, :, 2:]], axis=4)

    kb, vb = band(k), band(v)
    rel = (jnp.arange(3 * w)[None, :] - w) - jnp.arange(w)[:, None]
    key_idx = jnp.arange(nb)[:, None] * w - w + jnp.arange(3 * w)[None, :]
    valid = (jnp.abs(rel) <= w)[None] & ((key_idx >= 0) & (key_idx < L))[:, None, :]
    dist = (d * jnp.abs(rel)).astype(jnp.float32)

    s = jnp.einsum('bzhnqd,bzhnkd->bzhnqk', qb, kb).astype(jnp.float32) * (Dh ** -0.5)
    s = s - slopes[:, None, None, None] * dist
    s = jnp.where(valid, s, NEG_INF)
    m = jnp.max(s, axis=-1, keepdims=True)
    p = jnp.exp(s - m)
    den = jnp.sum(p, axis=-1, keepdims=True)
    o = jnp.einsum('bzhnqk,bzhnkd->bzhnqd', p, vb.astype(jnp.float32)) / den
    lse = (m + jnp.log(den))[..., 0]

    o = o.reshape(B, d, H, Lp, Dh)[:, :, :, :L].transpose(0, 3, 1, 2, 4).reshape(B, S, H, Dh)
    lse = lse.reshape(B, d, H, Lp)[:, :, :, :L].transpose(0, 3, 1, 2).reshape(B, S, H)
    return o, lse


def mixer_dilated_attention(h, w_in, w_out, slopes):
    B, S, _ = h.shape
    z = h @ w_in
    qkv = z[..., :ATTN_QKV_COLS].reshape(B, S, N_ATTN_GROUPS, 3, HEADS_PER_GROUP, HEAD_DIM)
    gate = z[..., ATTN_QKV_COLS:]
    outs, lses = [], []
    for g, (window, dil) in enumerate(ATTN_GROUPS):
        o, l = dilated_window_attention(qkv[:, :, g, 0], qkv[:, :, g, 1], qkv[:, :, g, 2],
                                        slopes[g], window, dil)
        outs.append(o)
        lses.append(l)
    wts = jax.nn.softmax(jnp.stack(lses, axis=0), axis=0)
    o = jnp.sum(wts[..., None] * jnp.stack(outs, axis=0), axis=0)
    y = o.reshape(B, S, ATTN_WIDTH).astype(h.dtype) * jax.nn.silu(gate)
    return y @ w_out


def mixer_spatial_gating(h, w_in, ln_g, ln_b, w_s, b_s, w_out):
    B, S, _ = h.shape
    E = SGU_WIDTH
    z = h @ w_in
    uv = jax.nn.gelu(z[..., :2 * E], approximate=False)
    gate = z[..., 2 * E:]
    u, v = uv[..., :E], uv[..., E:]
    v = layer_norm(v, ln_g, ln_b)
    vc = v.reshape(B, S // SGU_CHUNK, SGU_CHUNK, SGU_GROUPS, SGU_GROUP_CH)
    sv = jnp.einsum('gts,bnsgc->bntgc', w_s, vc) + b_s.T[None, None, :, :, None]
    y = u * sv.reshape(B, S, E) * jax.nn.silu(gate)
    return y @ w_out


def setup_inputs(seed: int = 0) -> dict:
    key = jax.random.key(seed)
    ks = jax.random.split(key, 16)
    D, E, AW = D_MODEL, SGU_WIDTH, ATTN_WIDTH
    nrm = jax.random.normal
    f32 = jnp.float32
    return {
        "x": nrm(ks[0], (BATCH, SEQ, D), f32),
        "c": nrm(ks[1], (BATCH, D), f32),
        "ada_w": nrm(ks[2], (DEPTH, D, 3 * D), f32) * (0.5 * D ** -0.5),
        "ada_b": nrm(ks[3], (DEPTH, 3 * D), f32) * 0.02,
        "post_ln_g": 1.0 + 0.05 * nrm(ks[4], (DEPTH, D), f32),
        "post_ln_b": 0.02 * nrm(ks[5], (DEPTH, D), f32),
        "a_w_in": nrm(ks[6], (N_A_LAYERS, D, ATTN_QKV_COLS + AW), f32) * D ** -0.5,
        "a_w_out": nrm(ks[7], (N_A_LAYERS, AW, D), f32) * (AW ** -0.5 * DEEPNORM_BETA),
        "b_w_in": nrm(ks[8], (N_B_LAYERS, D, 3 * E), f32) * D ** -0.5,
        "b_ln_g": 1.0 + 0.05 * nrm(ks[9], (N_B_LAYERS, E), f32),
        "b_ln_b": 0.02 * nrm(ks[10], (N_B_LAYERS, E), f32),
        "b_w_s": nrm(ks[11], (N_B_LAYERS, SGU_GROUPS, SGU_CHUNK, SGU_CHUNK), f32) * SGU_CHUNK ** -0.5,
        "b_b_s": 1.0 + 0.1 * nrm(ks[12], (N_B_LAYERS, SGU_GROUPS, SGU_CHUNK), f32),
        "b_w_out": nrm(ks[13], (N_B_LAYERS, E, D), f32) * (E ** -0.5 * DEEPNORM_BETA),
    }


def reference(x, c, ada_w, ada_b, post_ln_g, post_ln_b, a_w_in, a_w_out,
              b_w_in, b_ln_g, b_ln_b, b_w_s, b_b_s, b_w_out):
    slopes = alibi_slopes(N_ATTN_GROUPS * HEADS_PER_GROUP).reshape(N_ATTN_GROUPS, HEADS_PER_GROUP)
    cond = jax.nn.silu(c)
    for i in range(DEPTH):
        mod = cond @ ada_w[i] + ada_b[i]
        shift, scale, gate = jnp.split(mod, 3, axis=-1)
        h = x * (1.0 + scale[:, None, :]) + shift[:, None, :]
        j = i // N_MIXERS
        if i % N_MIXERS == 0:
            y = mixer_dilated_attention(h, a_w_in[j], a_w_out[j], slopes)
        else:
            y = mixer_spatial_gating(h, b_w_in[j], b_ln_g[j], b_ln_b[j], b_w_s[j], b_b_s[j], b_w_out[j])
        x = layer_norm(DEEPNORM_ALPHA * x + gate[:, None, :] * y, post_ln_g[i], post_ln_b[i])
    return x
```

```python
import functools

import numpy as np
import jax
import jax.numpy as jnp
from jax import lax
from jax.experimental import pallas as pl
from jax.experimental.pallas import tpu as pltpu

F32 = jnp.float32
BF16 = jnp.bfloat16

DEPTH = 2
ATTN_GROUPS = ((128, 1), (512, 4), (2048, 16))
N_GROUPS = len(ATTN_GROUPS)
DILATIONS = tuple(d for _, d in ATTN_GROUPS)
HEADS = 16
HEAD_DIM = 64
HALF_WIN = 64
SGU_CHUNK = 128
SGU_GROUP_CH = 128
DEEPNORM_ALPHA = (2 * DEPTH) ** 0.25
LN_EPS = 1e-5
NEG_INF = -1e30
LANES = 128

TOKEN_BLOCK = 1024
HEADS_PER_STEP = 4
COL_W = HEADS_PER_STEP * HEAD_DIM
Q_CHUNK = 64
K_TILE = Q_CHUNK + 2 * HALF_WIN
VMEM_LIMIT = 56 * 1024 * 1024

assert all(w // (2 * d) == HALF_WIN for w, d in ATTN_GROUPS)


def _silu(x):
    return x * jax.nn.sigmoid(x)


def _layer_norm_rows(x, g, b):
    mu = jnp.mean(x, axis=-1, keepdims=True)
    xc = x - mu
    var = jnp.mean(xc * xc, axis=-1, keepdims=True)
    return xc * lax.rsqrt(var + LN_EPS) * g + b


def _mod_kernel(ct_ref, w_ref, b_ref, o_ref):
    ct = ct_ref[...]
    cond = _silu(ct)
    w = w_ref[...]
    for b in range(ct.shape[1]):
        col = cond[:, b:b + 1]
        o_ref[b:b + 1, :] = jnp.sum(col * w, axis=0, keepdims=True) + b_ref[...]


def _modulation(c, ada_w, ada_b):
    B, D = c.shape
    depth, _, n3 = ada_w.shape
    tn = 512
    return pl.pallas_call(
        _mod_kernel,
        grid=(depth, n3 // tn),
        in_specs=[
            pl.BlockSpec((D, B), lambda i, n: (0, 0)),
            pl.BlockSpec((None, D, tn), lambda i, n: (i, 0, n)),
            pl.BlockSpec((None, 1, tn), lambda i, n: (i, 0, n)),
        ],
        out_specs=pl.BlockSpec((None, B, tn), lambda i, n: (i, 0, n)),
        out_shape=jax.ShapeDtypeStruct((depth, B, n3), F32),
        compiler_params=pltpu.CompilerParams(
            dimension_semantics=("arbitrary", "arbitrary")),
        name="adaln_mod",
    )(c.T, ada_w, ada_b.reshape(depth, 1, n3))


def _inproj_kernel(x_ref, sc_ref, sh_ref, w_ref, z_ref, h_scr, xs_scr):
    n = pl.program_id(2)
    tm, D = x_ref.shape

    @pl.when(n == 0)
    def _():
        h_scr[0] = (x_ref[...] * sc_ref[...] + sh_ref[...]).astype(BF16)
        for cc in range(D // LANES):
            cs = slice(cc * LANES, (cc + 1) * LANES)
            xs_scr[cc] = x_ref[:, cs]
        for g, d in enumerate(DILATIONS):
            if d == 1:
                continue
            rows = tm // d
            for cc in range(D // LANES):
                cs = slice(cc * LANES, (cc + 1) * LANES)
                sc = sc_ref[:, cs]
                sh = sh_ref[:, cs]
                for r in range(d):
                    xr = xs_scr[cc, pl.ds(r, rows, stride=d), :]
                    h_scr[g, r * rows:(r + 1) * rows, cs] = (xr * sc + sh).astype(BF16)

    is_gate = n == 3 * N_GROUPS
    g_sel = jnp.where(is_gate, 0, n // 3)
    z = jnp.dot(h_scr[g_sel], w_ref[...], preferred_element_type=F32)
    mult = jnp.where(jnp.logical_and(n % 3 == 0, jnp.logical_not(is_gate)),
                     HEAD_DIM ** -0.5, 1.0).astype(F32)
    z = (z * mult).astype(BF16)
    for cc in range(z_ref.shape[0]):
        z_ref[cc] = z[:, cc * COL_W:(cc + 1) * COL_W]


def _attn_inproj(x, sc, sh, w_bf16):
    B, S, D = x.shape
    P = TOKEN_BLOCK
    nblk = S // P
    ncols = w_bf16.shape[1] // D
    ncw = D // COL_W
    return pl.pallas_call(
        _inproj_kernel,
        grid=(B, nblk, ncols),
        in_specs=[
            pl.BlockSpec((None, P, D), lambda b, i, n: (b, i, 0)),
            pl.BlockSpec((None, 1, D), lambda b, i, n: (b, 0, 0)),
            pl.BlockSpec((None, 1, D), lambda b, i, n: (b, 0, 0)),
            pl.BlockSpec((D, D), lambda b, i, n: (0, n)),
        ],
        out_specs=pl.BlockSpec((None, None, None, ncw, P, COL_W),
                               lambda b, i, n: (n, b, i, 0, 0, 0)),
        out_shape=jax.ShapeDtypeStruct((ncols, B, nblk, ncw, P, COL_W), BF16),
        scratch_shapes=[pltpu.VMEM((N_GROUPS, P, D), BF16),
                        pltpu.VMEM((D // LANES, P, LANES), F32)],
        compiler_params=pltpu.CompilerParams(
            dimension_semantics=("arbitrary", "arbitrary", "arbitrary"),
            vmem_limit_bytes=VMEM_LIMIT),
        name="attn_inproj",
    )(x, sc, sh, w_bf16)


def _attn_kernel(slopes_ref, *refs):
    P = TOKEN_BLOCK
    group_refs = [refs[7 * g:7 * g + 7] for g in range(N_GROUPS)]
    gate_ref = refs[7 * N_GROUPS]
    y_ref = refs[7 * N_GROUPS + 1]
    scr = refs[7 * N_GROUPS + 2:]
    kv_scr = [scr[2 * g:2 * g + 2] for g in range(N_GROUPS)]
    bias_scr, o_scr, l_scr = scr[2 * N_GROUPS:]

    i = pl.program_id(1)
    hq = pl.program_id(2)
    nblk = pl.num_programs(1)

    lane = lax.broadcasted_iota(jnp.int32, (1, COL_W), 1)
    head_masks = [(lane >= h * HEAD_DIM) & (lane < (h + 1) * HEAD_DIM)
                  for h in range(HEADS_PER_STEP)]
    qq = lax.broadcasted_iota(jnp.int32, (Q_CHUNK, K_TILE), 0)
    kk = lax.broadcasted_iota(jnp.int32, (Q_CHUNK, K_TILE), 1)
    rel = jnp.abs(kk - HALF_WIN - qq)
    dist = rel.astype(F32)
    band = jnp.where(rel <= HALF_WIN, 0.0, NEG_INF).astype(F32)
    kcol = lax.broadcasted_iota(jnp.int32, (1, K_TILE), 1)
    lo_cols = (kcol < HALF_WIN).astype(F32)
    hi_cols = (kcol >= K_TILE - HALF_WIN).astype(F32)

    for g, d in enumerate(DILATIONS):
        q_ref, k_ref, v_ref, kp_ref, kn_ref, vp_ref, vn_ref = group_refs[g]
        kext, vext = kv_scr[g]
        rows = P // d
        nc = rows // Q_CHUNK

        kext[:, 0:HALF_WIN, :] = kp_ref[...]
        kext[:, HALF_WIN:HALF_WIN + rows, :] = k_ref[...]
        kext[:, HALF_WIN + rows:, :] = kn_ref[...]
        vext[:, 0:HALF_WIN, :] = vp_ref[...]
        vext[:, HALF_WIN:HALF_WIN + rows, :] = v_ref[...]
        vext[:, HALF_WIN + rows:, :] = vn_ref[...]

        for h in range(HEADS_PER_STEP):
            slope = slopes_ref[g * HEADS + hq * HEADS_PER_STEP + h] * float(d)
            bias_scr[h] = band - slope * dist

        def unit(u, carry, g=g, d=d, rows=rows, nc=nc, q_ref=q_ref, kext=kext, vext=vext):
            r = u // nc
            c = u % nc
            off = pl.multiple_of(c * Q_CHUNK, Q_CHUNK)
            q = q_ref[r, pl.ds(off, Q_CHUNK), :]
            kt = kext[r, pl.ds(off, K_TILE), :]
            vt = vext[r, pl.ds(off, K_TILE), :]
            pen_lo = jnp.where((i == 0) & (c == 0), NEG_INF, 0.0).astype(F32)
            pen_hi = jnp.where((i == nblk - 1) & (c == nc - 1), NEG_INF, 0.0).astype(F32)
            pen = lo_cols * pen_lo + hi_cols * pen_hi

            acc = jnp.zeros((Q_CHUNK, COL_W), F32)
            inv_t = jnp.zeros((Q_CHUNK, COL_W), F32)
            lse_t = jnp.zeros((Q_CHUNK, COL_W), F32)
            for h in range(HEADS_PER_STEP):
                hm = head_masks[h]
                qm = jnp.where(hm, q, jnp.zeros_like(q))
                s = lax.dot_general(qm, kt, (((1,), (1,)), ((), ())),
                                    preferred_element_type=F32)
                s = s + bias_scr[h] + pen
                m = jnp.max(s, axis=1, keepdims=True)
                p = jnp.exp(s - m)
                den = jnp.sum(p, axis=1, keepdims=True)
                vm = jnp.where(hm, vt, jnp.zeros_like(vt))
                acc = acc + jnp.dot(p.astype(BF16), vm, preferred_element_type=F32)
                inv_t = jnp.where(hm, 1.0 / den, inv_t)
                lse_t = jnp.where(hm, m + jnp.log(den), lse_t)
            o = acc * inv_t
            tok0 = c * (Q_CHUNK * d) + r
            if d == 1:
                dst = pl.ds(pl.multiple_of(tok0, Q_CHUNK), Q_CHUNK)
            else:
                dst = pl.ds(tok0, Q_CHUNK, stride=d)
            for cc in range(COL_W // LANES):
                cs = slice(cc * LANES, (cc + 1) * LANES)
                o_scr[g, cc, dst, :] = o[:, cs]
                l_scr[g, cc, dst, :] = lse_t[:, cs]
            return carry

        lax.fori_loop(0, P // Q_CHUNK, unit, 0)

    mrows = 128

    def merge(t, carry):
        sl = pl.ds(pl.multiple_of(t * mrows, mrows), mrows)
        for cc in range(COL_W // LANES):
            cs = slice(cc * LANES, (cc + 1) * LANES)
            l0, l1, l2 = l_scr[0, cc, sl, :], l_scr[1, cc, sl, :], l_scr[2, cc, sl, :]
            mx = jnp.maximum(jnp.maximum(l0, l1), l2)
            e0, e1, e2 = jnp.exp(l0 - mx), jnp.exp(l1 - mx), jnp.exp(l2 - mx)
            inv = 1.0 / (e0 + e1 + e2)
            o = ((e0 * inv) * o_scr[0, cc, sl, :] + (e1 * inv) * o_scr[1, cc, sl, :]
                 + (e2 * inv) * o_scr[2, cc, sl, :])
            gt = gate_ref[sl, cs].astype(F32)
            y_ref[sl, cs] = (o * _silu(gt)).astype(BF16)
        return carry

    lax.fori_loop(0, P // mrows, merge, 0)


def _attention(z, slopes, B, S, D):
    P = TOKEN_BLOCK
    nblk = S // P
    ncw = D // COL_W
    ncols = z.shape[0]
    in_specs = []
    operands = []
    scratch = []
    for g, d in enumerate(DILATIONS):
        rows = P // d
        nh = rows // HALF_WIN
        zg = z.reshape(ncols, B, nblk, ncw, d, rows, COL_W)
        zh = z.reshape(ncols, B, nblk, ncw, d, nh, HALF_WIN, COL_W)
        full = (None, None, None, None, d, rows, COL_W)
        halo = (None, None, None, None, d, None, HALF_WIN, COL_W)

        def cur(n):
            return pl.BlockSpec(full, lambda b, i, hq, s, n=n: (n, b, i, hq, 0, 0, 0))

        def prev(n, nh=nh):
            return pl.BlockSpec(halo, lambda b, i, hq, s, n=n, nh=nh:
                                (n, b, jnp.maximum(i - 1, 0), hq, 0, nh - 1, 0, 0))

        def nxt(n, nblk=nblk):
            return pl.BlockSpec(halo, lambda b, i, hq, s, n=n, nblk=nblk:
                                (n, b, jnp.minimum(i + 1, nblk - 1), hq, 0, 0, 0, 0))

        nq, nk, nv = 3 * g, 3 * g + 1, 3 * g + 2
        in_specs += [cur(nq), cur(nk), cur(nv), prev(nk), nxt(nk), prev(nv), nxt(nv)]
        operands += [zg, zg, zg, zh, zh, zh, zh]
        scratch += [pltpu.VMEM((d, rows + 2 * HALF_WIN, COL_W), BF16)] * 2
    zgate = z.reshape(ncols, B, nblk, ncw, P, COL_W)
    in_specs.append(pl.BlockSpec((None, None, None, None, P, COL_W),
                                 lambda b, i, hq, s: (ncols - 1, b, i, hq, 0, 0)))
    operands.append(zgate)
    scratch += [
        pltpu.VMEM((HEADS_PER_STEP, Q_CHUNK, K_TILE), F32),
        pltpu.VMEM((N_GROUPS, COL_W // LANES, P, LANES), F32),
        pltpu.VMEM((N_GROUPS, COL_W // LANES, P, LANES), F32),
    ]
    grid_spec = pltpu.PrefetchScalarGridSpec(
        num_scalar_prefetch=1,
        grid=(B, nblk, ncw),
        in_specs=in_specs,
        out_specs=pl.BlockSpec((None, P, COL_W), lambda b, i, hq, s: (b, i, hq)),
        scratch_shapes=scratch,
    )
    return pl.pallas_call(
        _attn_kernel,
        grid_spec=grid_spec,
        out_shape=jax.ShapeDtypeStruct((B, S, D), BF16),
        compiler_params=pltpu.CompilerParams(
            dimension_semantics=("arbitrary", "arbitrary", "arbitrary"),
            vmem_limit_bytes=VMEM_LIMIT),
        name="dilated_attn",
    )(slopes, *operands)


def _outproj_kernel(y_ref, x_ref, gc_ref, w_ref, lg_ref, lb_ref, o_ref):
    out = jnp.dot(y_ref[...], w_ref[...], preferred_element_type=F32)
    res = DEEPNORM_ALPHA * x_ref[...] + gc_ref[...] * out
    o_ref[...] = _layer_norm_rows(res, lg_ref[...], lb_ref[...])


def _outproj_ln(y, x, gate_c, w_bf16, ln_g, ln_b):
    B, S, D = x.shape
    K = y.shape[-1]
    tm = 512
    return pl.pallas_call(
        _outproj_kernel,
        grid=(B, S // tm),
        in_specs=[
            pl.BlockSpec((None, tm, K), lambda b, i: (b, i, 0)),
            pl.BlockSpec((None, tm, D), lambda b, i: (b, i, 0)),
            pl.BlockSpec((None, 1, D), lambda b, i: (b, 0, 0)),
            pl.BlockSpec((K, D), lambda b, i: (0, 0)),
            pl.BlockSpec((1, D), lambda b, i: (0, 0)),
            pl.BlockSpec((1, D), lambda b, i: (0, 0)),
        ],
        out_specs=pl.BlockSpec((None, tm, D), lambda b, i: (b, i, 0)),
        out_shape=jax.ShapeDtypeStruct((B, S, D), F32),
        compiler_params=pltpu.CompilerParams(
            dimension_semantics=("arbitrary", "arbitrary"),
            vmem_limit_bytes=VMEM_LIMIT),
        name="attn_outproj_ln",
    )(y, x, gate_c, w_bf16, ln_g, ln_b)


SGU_COLS = 256


def _gelu_exact(x):
    return 0.5 * x * (1.0 + lax.erf(x * (2.0 ** -0.5)))


def _sgu_kernel(x_ref, sc_ref, sh_ref, gc_ref, win_ref, lng_ref, lnb_ref, ws_ref, bs_ref,
                wout_ref, plg_ref, plb_ref, o_ref, h_scr, v_scr, y_scr):
    tm = x_ref.shape[0]
    E = v_scr.shape[1]
    x = x_ref[...]
    h_scr[...] = (x * sc_ref[...] + sh_ref[...]).astype(BF16)

    zv = jnp.dot(h_scr[...], win_ref[:, E:2 * E], preferred_element_type=F32)
    v = _layer_norm_rows(_gelu_exact(zv), lng_ref[...], lnb_ref[...])
    v_scr[...] = v.astype(BF16)

    gpb = SGU_COLS // SGU_GROUP_CH
    for cb in range(E // SGU_COLS):
        c0 = cb * SGU_COLS
        zu = jnp.dot(h_scr[...], win_ref[:, c0:c0 + SGU_COLS], preferred_element_type=F32)
        zg = jnp.dot(h_scr[...], win_ref[:, 2 * E + c0:2 * E + c0 + SGU_COLS],
                     preferred_element_type=F32)
        ug = _gelu_exact(zu) * _silu(zg)
        for gi in range(gpb):
            g = cb * gpb + gi
            l0 = gi * SGU_GROUP_CH
            wsg = ws_ref[g]
            bsg = bs_ref[:, g:g + 1]
            for nn in range(tm // SGU_CHUNK):
                r0 = nn * SGU_CHUNK
                vc = v_scr[r0:r0 + SGU_CHUNK, c0 + l0:c0 + l0 + SGU_GROUP_CH]
                sv = jnp.dot(wsg, vc, preferred_element_type=F32) + bsg
                y = ug[r0:r0 + SGU_CHUNK, l0:l0 + SGU_GROUP_CH] * sv
                y_scr[r0:r0 + SGU_CHUNK, c0 + l0:c0 + l0 + SGU_GROUP_CH] = y.astype(BF16)

    out = jnp.dot(y_scr[...], wout_ref[...], preferred_element_type=F32)
    res = DEEPNORM_ALPHA * x + gc_ref[...] * out
    o_ref[...] = _layer_norm_rows(res, plg_ref[...], plb_ref[...])


def _sgu_layer(x, sc, sh, gate_c, w_in, ln_g, ln_b, w_s, b_s_t, w_out, post_g, post_b):
    B, S, D = x.shape
    E = w_out.shape[0]
    G = w_s.shape[0]
    tm = 256
    const2 = lambda b, i: (0, 0)
    single = pl.Buffered(1)
    return pl.pallas_call(
        _sgu_kernel,
        grid=(B, S // tm),
        in_specs=[
            pl.BlockSpec((None, tm, D), lambda b, i: (b, i, 0)),
            pl.BlockSpec((None, 1, D), lambda b, i: (b, 0, 0)),
            pl.BlockSpec((None, 1, D), lambda b, i: (b, 0, 0)),
            pl.BlockSpec((None, 1, D), lambda b, i: (b, 0, 0)),
            pl.BlockSpec((D, 3 * E), const2, pipeline_mode=single),
            pl.BlockSpec((1, E), const2),
            pl.BlockSpec((1, E), const2),
            pl.BlockSpec((G, SGU_CHUNK, SGU_CHUNK), lambda b, i: (0, 0, 0), pipeline_mode=single),
            pl.BlockSpec((SGU_CHUNK, G), const2),
            pl.BlockSpec((E, D), const2, pipeline_mode=single),
            pl.BlockSpec((1, D), const2),
            pl.BlockSpec((1, D), const2),
        ],
        out_specs=pl.BlockSpec((None, tm, D), lambda b, i: (b, i, 0)),
        out_shape=jax.ShapeDtypeStruct((B, S, D), F32),
        scratch_shapes=[
            pltpu.VMEM((tm, D), BF16),
            pltpu.VMEM((tm, E), BF16),
            pltpu.VMEM((tm, E), BF16),
        ],
        compiler_params=pltpu.CompilerParams(
            dimension_semantics=("arbitrary", "arbitrary"),
            vmem_limit_bytes=VMEM_LIMIT),
        name="sgu_layer",
    )(x, sc, sh, gate_c, w_in, ln_g, ln_b, w_s, b_s_t, w_out, post_g, post_b)


def _alibi_slopes(n):
    return 2.0 ** (-8.0 * jnp.arange(1, n + 1, dtype=F32) / n)


def kernel(x, c, ada_w, ada_b, post_ln_g, post_ln_b, a_w_in, a_w_out, b_w_in, b_ln_g, b_ln_b,
           b_w_s, b_b_s, b_w_out):
    B, S, D = x.shape
    mod = _modulation(c, ada_w, ada_b)
    slopes = _alibi_slopes(N_GROUPS * HEADS)

    def mod_parts(i):
        shift = mod[i, :, None, 0:D]
        scale1 = 1.0 + mod[i, :, None, D:2 * D]
        gate_c = mod[i, :, None, 2 * D:3 * D]
        return scale1, shift, gate_c

    sc, sh, gc = mod_parts(0)
    z = _attn_inproj(x, sc, sh, a_w_in[0].astype(BF16))
    y = _attention(z, slopes, B, S, D)
    x = _outproj_ln(y, x, gc, a_w_out[0].astype(BF16),
                    post_ln_g[0][None, :], post_ln_b[0][None, :])

    sc, sh, gc = mod_parts(1)
    x = _sgu_layer(x, sc, sh, gc, b_w_in[0].astype(BF16), b_ln_g[0][None, :], b_ln_b[0][None, :],
                   b_w_s[0].astype(BF16), b_b_s[0].T, b_w_out[0].astype(BF16),
                   post_ln_g[1][None, :], post_ln_b[1][None, :])
    return x
```

```python
import functools

import numpy as np
import jax
import jax.numpy as jnp
from jax import lax
from jax.experimental import pallas as pl
from jax.experimental.pallas import tpu as pltpu

F32 = jnp.float32
BF16 = jnp.bfloat16

DEPTH = 2
ATTN_GROUPS = ((128, 1), (512, 4), (2048, 16))
N_GROUPS = len(ATTN_GROUPS)
DILATIONS = tuple(d for _, d in ATTN_GROUPS)
HEADS = 16
HEAD_DIM = 64
HALF_WIN = 64
SGU_CHUNK = 128
SGU_GROUP_CH = 128
DEEPNORM_ALPHA = (2 * DEPTH) ** 0.25
LN_EPS = 1e-5
NEG_INF = -1e30
LANES = 128

TOKEN_BLOCK = 1024
HEADS_PER_STEP = 4
COL_W = HEADS_PER_STEP * HEAD_DIM
Q_CHUNK = 64
K_TILE = Q_CHUNK + 2 * HALF_WIN
VMEM_LIMIT = 56 * 1024 * 1024

assert all(w // (2 * d) == HALF_WIN for w, d in ATTN_GROUPS)


def _silu(x):
    return x * jax.nn.sigmoid(x)


def _layer_norm_rows(x, g, b):
    mu = jnp.mean(x, axis=-1, keepdims=True)
    xc = x - mu
    var = jnp.mean(xc * xc, axis=-1, keepdims=True)
    return xc * lax.rsqrt(var + LN_EPS) * g + b


def _mod_kernel(ct_ref, w_ref, b_ref, o_ref):
    ct = ct_ref[...]
    cond = _silu(ct)
    w = w_ref[...]
    for b in range(ct.shape[1]):
        col = cond[:, b:b + 1]
        o_ref[b:b + 1, :] = jnp.sum(col * w, axis=0, keepdims=True) + b_ref[...]


def _modulation(c, ada_w, ada_b):
    B, D = c.shape
    depth, _, n3 = ada_w.shape
    tn = 512
    return pl.pallas_call(
        _mod_kernel,
        grid=(depth, n3 // tn),
        in_specs=[
            pl.BlockSpec((D, B), lambda i, n: (0, 0)),
            pl.BlockSpec((None, D, tn), lambda i, n: (i, 0, n)),
            pl.BlockSpec((None, 1, tn), lambda i, n: (i, 0, n)),
        ],
        out_specs=pl.BlockSpec((None, B, tn), lambda i, n: (i, 0, n)),
        out_shape=jax.ShapeDtypeStruct((depth, B, n3), F32),
        compiler_params=pltpu.CompilerParams(
            dimension_semantics=("arbitrary", "arbitrary")),
        name="adaln_mod",
    )(c.T, ada_w, ada_b.reshape(depth, 1, n3))


def _inproj_kernel(x_ref, sc_ref, sh_ref, w_ref, z_ref, h_scr, xs_scr):
    n = pl.program_id(2)
    tm, D = x_ref.shape

    @pl.when(n == 0)
    def _():
        h_scr[0] = (x_ref[...] * sc_ref[...] + sh_ref[...]).astype(BF16)
        for cc in range(D // LANES):
            cs = slice(cc * LANES, (cc + 1) * LANES)
            xs_scr[cc] = x_ref[:, cs]
        for g, d in enumerate(DILATIONS):
            if d == 1:
                continue
            rows = tm // d
            for cc in range(D // LANES):
                cs = slice(cc * LANES, (cc + 1) * LANES)
                sc = sc_ref[:, cs]
                sh = sh_ref[:, cs]
                for r in range(d):
                    xr = xs_scr[cc, pl.ds(r, rows, stride=d), :]
                    h_scr[g, r * rows:(r + 1) * rows, cs] = (xr * sc + sh).astype(BF16)

    is_gate = n == 3 * N_GROUPS
    g_sel = jnp.where(is_gate, 0, n // 3)
    z = jnp.dot(h_scr[g_sel], w_ref[...], preferred_element_type=F32)
    mult = jnp.where(jnp.logical_and(n % 3 == 0, jnp.logical_not(is_gate)),
                     HEAD_DIM ** -0.5, 1.0).astype(F32)
    z = (z * mult).astype(BF16)
    for cc in range(z_ref.shape[0]):
        z_ref[cc] = z[:, cc * COL_W:(cc + 1) * COL_W]


def _attn_inproj(x, sc, sh, w_bf16):
    B, S, D = x.shape
    P = TOKEN_BLOCK
    nblk = S // P
    ncols = w_bf16.shape[1] // D
    ncw = D // COL_W
    return pl.pallas_call(
        _inproj_kernel,
        grid=(B, nblk, ncols),
        in_specs=[
            pl.BlockSpec((None, P, D), lambda b, i, n: (b, i, 0)),
            pl.BlockSpec((None, 1, D), lambda b, i, n: (b, 0, 0)),
            pl.BlockSpec((None, 1, D), lambda b, i, n: (b, 0, 0)),
            pl.BlockSpec((D, D), lambda b, i, n: (0, n)),
        ],
        out_specs=pl.BlockSpec((None, None, None, ncw, P, COL_W),
                               lambda b, i, n: (n, b, i, 0, 0, 0)),
        out_shape=jax.ShapeDtypeStruct((ncols, B, nblk, ncw, P, COL_W), BF16),
        scratch_shapes=[pltpu.VMEM((N_GROUPS, P, D), BF16),
                        pltpu.VMEM((D // LANES, P, LANES), F32)],
        compiler_params=pltpu.CompilerParams(
            dimension_semantics=("arbitrary", "arbitrary", "arbitrary"),
            vmem_limit_bytes=VMEM_LIMIT),
        name="attn_inproj",
    )(x, sc, sh, w_bf16)


def _attn_kernel(slopes_ref, *refs):
    P = TOKEN_BLOCK
    group_refs = [refs[7 * g:7 * g + 7] for g in range(N_GROUPS)]
    gate_ref = refs[7 * N_GROUPS]
    y_ref = refs[7 * N_GROUPS + 1]
    scr = refs[7 * N_GROUPS + 2:]
    kv_scr = [scr[2 * g:2 * g + 2] for g in range(N_GROUPS)]
    bias_scr, s_a, s_b, p_a, p_b, o_scr, l_scr = scr[2 * N_GROUPS:]

    i = pl.program_id(1)
    hq = pl.program_id(2)
    nblk = pl.num_programs(1)

    lane = lax.broadcasted_iota(jnp.int32, (1, COL_W), 1)
    head_masks = [(lane >= h * HEAD_DIM) & (lane < (h + 1) * HEAD_DIM)
                  for h in range(HEADS_PER_STEP)]
    qq = lax.broadcasted_iota(jnp.int32, (Q_CHUNK, K_TILE), 0)
    kk = lax.broadcasted_iota(jnp.int32, (Q_CHUNK, K_TILE), 1)
    rel = jnp.abs(kk - HALF_WIN - qq)
    dist = rel.astype(F32)
    band = jnp.where(rel <= HALF_WIN, 0.0, NEG_INF).astype(F32)
    lo_pen = jnp.where(kk < HALF_WIN, NEG_INF, 0.0).astype(F32)
    hi_pen = jnp.where(kk >= K_TILE - HALF_WIN, NEG_INF, 0.0).astype(F32)
    pair_lo = lax.broadcasted_iota(jnp.int32, (1, LANES), 1) < HEAD_DIM
    nt_dims = (((1,), (1,)), ((), ()))
    n_units = P // Q_CHUNK
    n_pairs = n_units // 2

    for g, d in enumerate(DILATIONS):
        q_ref, k_ref, v_ref, kp_ref, kn_ref, vp_ref, vn_ref = group_refs[g]
        kext, vext = kv_scr[g]
        rows = P // d
        nc = rows // Q_CHUNK

        kext[:, 0:HALF_WIN, :] = kp_ref[...]
        kext[:, HALF_WIN:HALF_WIN + rows, :] = k_ref[...]
        kext[:, HALF_WIN + rows:, :] = kn_ref[...]
        vext[:, 0:HALF_WIN, :] = vp_ref[...]
        vext[:, HALF_WIN:HALF_WIN + rows, :] = v_ref[...]
        vext[:, HALF_WIN + rows:, :] = vn_ref[...]

        for h in range(HEADS_PER_STEP):
            slope = slopes_ref[g * HEADS + hq * HEADS_PER_STEP + h] * float(d)
            base = band - slope * dist
            bias_scr[0, h] = base
            bias_scr[1, h] = base + lo_pen
            bias_scr[2, h] = base + hi_pen

        def coords(u, d=d, nc=nc):
            r, c = u // nc, u % nc
            off = c * Q_CHUNK
            tok0 = c * (Q_CHUNK * d) + r
            if not isinstance(u, int):
                off = pl.multiple_of(off, Q_CHUNK)
            if d == 1:
                dst = pl.ds(off, Q_CHUNK)
            else:
                dst = pl.ds(tok0, Q_CHUNK, stride=d)
            return r, c, off, dst

        def qk(u, s_buf, q_ref=q_ref, kext=kext, coords=coords):
            r, _, off, _ = coords(u)
            q = q_ref[r, pl.ds(off, Q_CHUNK), :]
            kt = kext[r, pl.ds(off, K_TILE), :]
            zero = jnp.zeros_like(q)
            qs = jnp.concatenate([jnp.where(hm, q, zero) for hm in head_masks], axis=0)
            s_buf[...] = lax.dot_general(qs, kt, nt_dims, preferred_element_type=F32)

        def softmax(u, s_buf, p_buf, g=g, nc=nc, coords=coords):
            _, c, _, dst = coords(u)
            var = jnp.where((i == 0) & (c == 0), 1,
                            jnp.where((i == nblk - 1) & (c == nc - 1), 2, 0))
            lse_even = None
            for h in range(HEADS_PER_STEP):
                hs = slice(h * Q_CHUNK, (h + 1) * Q_CHUNK)
                s = s_buf[hs, :] + bias_scr[var, h]
                m = jnp.max(s, axis=1, keepdims=True)
                p = jnp.exp(s - m)
                den = jnp.sum(p, axis=1, keepdims=True)
                p_buf[hs, :] = (p * (1.0 / den)).astype(BF16)
                lse = m + jnp.log(den)
                if h % 2 == 0:
                    lse_even = lse
                else:
                    l_scr[g, h // 2, dst, :] = jnp.where(pair_lo, lse_even, lse)

        def pv(u, p_buf, g=g, vext=vext, coords=coords):
            r, _, off, dst = coords(u)
            vt = vext[r, pl.ds(off, K_TILE), :]
            oall = jnp.dot(p_buf[...], vt, preferred_element_type=F32)
            for k in range(HEADS_PER_STEP // 2):
                cs = slice(k * LANES, (k + 1) * LANES)
                even = oall[(2 * k) * Q_CHUNK:(2 * k + 1) * Q_CHUNK, cs]
                odd = oall[(2 * k + 1) * Q_CHUNK:(2 * k + 2) * Q_CHUNK, cs]
                o_scr[g, k, dst, :] = jnp.where(pair_lo, even, odd)

        def pair(it, first=False, last=False, qk=qk, softmax=softmax, pv=pv):
            a, b = 2 * it, 2 * it + 1
            if not first:
                pv(b - 2, p_b)
            qk(b, s_b)
            softmax(a, s_a, p_a)
            pv(a, p_a)
            if not last:
                qk(a + 2, s_a)
            softmax(b, s_b, p_b)

        qk(0, s_a)
        pair(0, first=True)

        def middle(it, carry, pair=pair):
            pair(it)
            return carry

        lax.fori_loop(1, n_pairs - 1, middle, 0)
        pair(n_pairs - 1, last=True)
        pv(n_units - 1, p_b)

    mrows = 128

    def merge(t, carry):
        sl = pl.ds(pl.multiple_of(t * mrows, mrows), mrows)
        for cc in range(COL_W // LANES):
            cs = slice(cc * LANES, (cc + 1) * LANES)
            l0, l1, l2 = l_scr[0, cc, sl, :], l_scr[1, cc, sl, :], l_scr[2, cc, sl, :]
            mx = jnp.maximum(jnp.maximum(l0, l1), l2)
            e0, e1, e2 = jnp.exp(l0 - mx), jnp.exp(l1 - mx), jnp.exp(l2 - mx)
            inv = 1.0 / (e0 + e1 + e2)
            o = ((e0 * inv) * o_scr[0, cc, sl, :] + (e1 * inv) * o_scr[1, cc, sl, :]
                 + (e2 * inv) * o_scr[2, cc, sl, :])
            gt = gate_ref[sl, cs].astype(F32)
            y_ref[sl, cs] = (o * _silu(gt)).astype(BF16)
        return carry

    lax.fori_loop(0, P // mrows, merge, 0)


def _attention(z, slopes, B, S, D):
    P = TOKEN_BLOCK
    nblk = S // P
    ncw = D // COL_W
    ncols = z.shape[0]
    in_specs = []
    operands = []
    scratch = []
    for g, d in enumerate(DILATIONS):
        rows = P // d
        nh = rows // HALF_WIN
        zg = z.reshape(ncols, B, nblk, ncw, d, rows, COL_W)
        zh = z.reshape(ncols, B, nblk, ncw, d, nh, HALF_WIN, COL_W)
        full = (None, None, None, None, d, rows, COL_W)
        halo = (None, None, None, None, d, None, HALF_WIN, COL_W)

        def cur(n):
            return pl.BlockSpec(full, lambda b, i, hq, s, n=n: (n, b, i, hq, 0, 0, 0))

        def prev(n, nh=nh):
            return pl.BlockSpec(halo, lambda b, i, hq, s, n=n, nh=nh:
                                (n, b, jnp.maximum(i - 1, 0), hq, 0, nh - 1, 0, 0))

        def nxt(n, nblk=nblk):
            return pl.BlockSpec(halo, lambda b, i, hq, s, n=n, nblk=nblk:
                                (n, b, jnp.minimum(i + 1, nblk - 1), hq, 0, 0, 0, 0))

        nq, nk, nv = 3 * g, 3 * g + 1, 3 * g + 2
        in_specs += [cur(nq), cur(nk), cur(nv), prev(nk), nxt(nk), prev(nv), nxt(nv)]
        operands += [zg, zg, zg, zh, zh, zh, zh]
        scratch += [pltpu.VMEM((d, rows + 2 * HALF_WIN, COL_W), BF16)] * 2
    zgate = z.reshape(ncols, B, nblk, ncw, P, COL_W)
    in_specs.append(pl.BlockSpec((None, None, None, None, P, COL_W),
                                 lambda b, i, hq, s: (ncols - 1, b, i, hq, 0, 0)))
    operands.append(zgate)
    scratch += [
        pltpu.VMEM((3, HEADS_PER_STEP, Q_CHUNK, K_TILE), F32),
        pltpu.VMEM((HEADS_PER_STEP * Q_CHUNK, K_TILE), F32),
        pltpu.VMEM((HEADS_PER_STEP * Q_CHUNK, K_TILE), F32),
        pltpu.VMEM((HEADS_PER_STEP * Q_CHUNK, K_TILE), BF16),
        pltpu.VMEM((HEADS_PER_STEP * Q_CHUNK, K_TILE), BF16),
        pltpu.VMEM((N_GROUPS, COL_W // LANES, P, LANES), F32),
        pltpu.VMEM((N_GROUPS, COL_W // LANES, P, LANES), F32),
    ]
    grid_spec = pltpu.PrefetchScalarGridSpec(
        num_scalar_prefetch=1,
        grid=(B, nblk, ncw),
        in_specs=in_specs,
        out_specs=pl.BlockSpec((None, P, COL_W), lambda b, i, hq, s: (b, i, hq)),
        scratch_shapes=scratch,
    )
    return pl.pallas_call(
        _attn_kernel,
        grid_spec=grid_spec,
        out_shape=jax.ShapeDtypeStruct((B, S, D), BF16),
        compiler_params=pltpu.CompilerParams(
            dimension_semantics=("arbitrary", "arbitrary", "arbitrary"),
            vmem_limit_bytes=VMEM_LIMIT),
        name="dilated_attn",
    )(slopes, *operands)


def _outproj_kernel(y_ref, x_ref, gc_ref, w_ref, lg_ref, lb_ref, o_ref):
    out = jnp.dot(y_ref[...], w_ref[...], preferred_element_type=F32)
    res = DEEPNORM_ALPHA * x_ref[...] + gc_ref[...] * out
    o_ref[...] = _layer_norm_rows(res, lg_ref[...], lb_ref[...])


def _outproj_ln(y, x, gate_c, w_bf16, ln_g, ln_b):
    B, S, D = x.shape
    K = y.shape[-1]
    tm = 512
    return pl.pallas_call(
        _outproj_kernel,
        grid=(B, S // tm),
        in_specs=[
            pl.BlockSpec((None, tm, K), lambda b, i: (b, i, 0)),
            pl.BlockSpec((None, tm, D), lambda b, i: (b, i, 0)),
            pl.BlockSpec((None, 1, D), lambda b, i: (b, 0, 0)),
            pl.BlockSpec((K, D), lambda b, i: (0, 0)),
            pl.BlockSpec((1, D), lambda b, i: (0, 0)),
            pl.BlockSpec((1, D), lambda b, i: (0, 0)),
        ],
        out_specs=pl.BlockSpec((None, tm, D), lambda b, i: (b, i, 0)),
        out_shape=jax.ShapeDtypeStruct((B, S, D), F32),
        compiler_params=pltpu.CompilerParams(
            dimension_semantics=("arbitrary", "arbitrary"),
            vmem_limit_bytes=VMEM_LIMIT),
        name="attn_outproj_ln",
    )(y, x, gate_c, w_bf16, ln_g, ln_b)


SGU_COLS = 256


def _gelu_exact(x):
    return 0.5 * x * (1.0 + lax.erf(x * (2.0 ** -0.5)))


def _sgu_kernel(x_ref, sc_ref, sh_ref, gc_ref, win_ref, lng_ref, lnb_ref, ws_ref, bs_ref,
                wout_ref, plg_ref, plb_ref, o_ref, h_scr, v_scr, y_scr):
    tm = x_ref.shape[0]
    E = v_scr.shape[1]
    x = x_ref[...]
    h_scr[...] = (x * sc_ref[...] + sh_ref[...]).astype(BF16)

    zv = jnp.dot(h_scr[...], win_ref[:, E:2 * E], preferred_element_type=F32)
    v = _layer_norm_rows(_gelu_exact(zv), lng_ref[...], lnb_ref[...])
    v_scr[...] = v.astype(BF16)

    gpb = SGU_COLS // SGU_GROUP_CH
    for cb in range(E // SGU_COLS):
        c0 = cb * SGU_COLS
        zu = jnp.dot(h_scr[...], win_ref[:, c0:c0 + SGU_COLS], preferred_element_type=F32)
        zg = jnp.dot(h_scr[...], win_ref[:, 2 * E + c0:2 * E + c0 + SGU_COLS],
                     preferred_element_type=F32)
        ug = _gelu_exact(zu) * _silu(zg)
        for gi in range(gpb):
            g = cb * gpb + gi
            l0 = gi * SGU_GROUP_CH
            wsg = ws_ref[g]
            bsg = bs_ref[:, g:g + 1]
            for nn in range(tm // SGU_CHUNK):
                r0 = nn * SGU_CHUNK
                vc = v_scr[r0:r0 + SGU_CHUNK, c0 + l0:c0 + l0 + SGU_GROUP_CH]
                sv = jnp.dot(wsg, vc, preferred_element_type=F32) + bsg
                y = ug[r0:r0 + SGU_CHUNK, l0:l0 + SGU_GROUP_CH] * sv
                y_scr[r0:r0 + SGU_CHUNK, c0 + l0:c0 + l0 + SGU_GROUP_CH] = y.astype(BF16)

    out = jnp.dot(y_scr[...], wout_ref[...], preferred_element_type=F32)
    res = DEEPNORM_ALPHA * x + gc_ref[...] * out
    o_ref[...] = _layer_norm_rows(res, plg_ref[...], plb_ref[...])


def _sgu_layer(x, sc, sh, gate_c, w_in, ln_g, ln_b, w_s, b_s_t, w_out, post_g, post_b):
    B, S, D = x.shape
    E = w_out.shape[0]
    G = w_s.shape[0]
    tm = 256
    const2 = lambda b, i: (0, 0)
    single = pl.Buffered(1)
    return pl.pallas_call(
        _sgu_kernel,
        grid=(B, S // tm),
        in_specs=[
            pl.BlockSpec((None, tm, D), lambda b, i: (b, i, 0)),
            pl.BlockSpec((None, 1, D), lambda b, i: (b, 0, 0)),
            pl.BlockSpec((None, 1, D), lambda b, i: (b, 0, 0)),
            pl.BlockSpec((None, 1, D), lambda b, i: (b, 0, 0)),
            pl.BlockSpec((D, 3 * E), const2, pipeline_mode=single),
            pl.BlockSpec((1, E), const2),
            pl.BlockSpec((1, E), const2),
            pl.BlockSpec((G, SGU_CHUNK, SGU_CHUNK), lambda b, i: (0, 0, 0), pipeline_mode=single),
            pl.BlockSpec((SGU_CHUNK, G), const2),
            pl.BlockSpec((E, D), const2, pipeline_mode=single),
            pl.BlockSpec((1, D), const2),
            pl.BlockSpec((1, D), const2),
        ],
        out_specs=pl.BlockSpec((None, tm, D), lambda b, i: (b, i, 0)),
        out_shape=jax.ShapeDtypeStruct((B, S, D), F32),
        scratch_shapes=[
            pltpu.VMEM((tm, D), BF16),
            pltpu.VMEM((tm, E), BF16),
            pltpu.VMEM((tm, E), BF16),
        ],
        compiler_params=pltpu.CompilerParams(
            dimension_semantics=("arbitrary", "arbitrary"),
            vmem_limit_bytes=VMEM_LIMIT),
        name="sgu_layer",
    )(x, sc, sh, gate_c, w_in, ln_g, ln_b, w_s, b_s_t, w_out, post_g, post_b)


def _alibi_slopes(n):
    return 2.0 ** (-8.0 * jnp.arange(1, n + 1, dtype=F32) / n)


def kernel(x, c, ada_w, ada_b, post_ln_g, post_ln_b, a_w_in, a_w_out, b_w_in, b_ln_g, b_ln_b,
           b_w_s, b_b_s, b_w_out):
    B, S, D = x.shape
    mod = _modulation(c, ada_w, ada_b)
    slopes = _alibi_slopes(N_GROUPS * HEADS)

    def mod_parts(i):
        shift = mod[i, :, None, 0:D]
        scale1 = 1.0 + mod[i, :, None, D:2 * D]
        gate_c = mod[i, :, None, 2 * D:3 * D]
        return scale1, shift, gate_c

    sc, sh, gc = mod_parts(0)
    z = _attn_inproj(x, sc, sh, a_w_in[0].astype(BF16))
    y = _attention(z, slopes, B, S, D)
    x = _outproj_ln(y, x, gc, a_w_out[0].astype(BF16),
                    post_ln_g[0][None, :], post_ln_b[0][None, :])

    sc, sh, gc = mod_parts(1)
    x = _sgu_layer(x, sc, sh, gc, b_w_in[0].astype(BF16), b_ln_g[0][None, :], b_ln_b[0][None, :],
                   b_w_s[0].astype(BF16), b_b_s[0].T, b_w_out[0].astype(BF16),
                   post_ln_g[1][None, :], post_ln_b[1][None, :])
    return x
```

```python
import functools

import numpy as np
import jax
import jax.numpy as jnp
from jax import lax
from jax.experimental import pallas as pl
from jax.experimental.pallas import tpu as pltpu

F32 = jnp.float32
BF16 = jnp.bfloat16

DEPTH = 2
ATTN_GROUPS = ((128, 1), (512, 4), (2048, 16))
N_GROUPS = len(ATTN_GROUPS)
DILATIONS = tuple(d for _, d in ATTN_GROUPS)
HEADS = 16
HEAD_DIM = 64
HALF_WIN = 64
SGU_CHUNK = 128
SGU_GROUP_CH = 128
DEEPNORM_ALPHA = (2 * DEPTH) ** 0.25
LN_EPS = 1e-5
NEG_INF = -1e30
LOG2E = 1.4426950408889634
LANES = 128

TOKEN_BLOCK = 1024
HEADS_PER_STEP = 4
COL_W = HEADS_PER_STEP * HEAD_DIM
Q_CHUNK = 64
K_TILE = Q_CHUNK + 2 * HALF_WIN
UNITS_PER_STEP = 2
BF16_ROWS = 16
SUBLANES = 8
SM_ROWS = 32
P_ROWS = 256
VMEM_LIMIT = 56 * 1024 * 1024

assert all(w // (2 * d) == HALF_WIN for w, d in ATTN_GROUPS)


def _silu(x):
    return x * jax.nn.sigmoid(x)


def _layer_norm_rows(x, g, b):
    mu = jnp.mean(x, axis=-1, keepdims=True)
    xc = x - mu
    var = jnp.mean(xc * xc, axis=-1, keepdims=True)
    return xc * lax.rsqrt(var + LN_EPS) * g + b


def _mod_kernel(ct_ref, w_ref, b_ref, o_ref):
    ct = ct_ref[...]
    cond = _silu(ct)
    w = w_ref[...]
    for b in range(ct.shape[1]):
        col = cond[:, b:b + 1]
        o_ref[b:b + 1, :] = jnp.sum(col * w, axis=0, keepdims=True) + b_ref[...]


def _modulation(c, ada_w, ada_b):
    B, D = c.shape
    depth, _, n3 = ada_w.shape
    tn = 512
    return pl.pallas_call(
        _mod_kernel,
        grid=(depth, n3 // tn),
        in_specs=[
            pl.BlockSpec((D, B), lambda i, n: (0, 0)),
            pl.BlockSpec((None, D, tn), lambda i, n: (i, 0, n)),
            pl.BlockSpec((None, 1, tn), lambda i, n: (i, 0, n)),
        ],
        out_specs=pl.BlockSpec((None, B, tn), lambda i, n: (i, 0, n)),
        out_shape=jax.ShapeDtypeStruct((depth, B, n3), F32),
        compiler_params=pltpu.CompilerParams(
            dimension_semantics=("arbitrary", "arbitrary")),
        name="adaln_mod",
    )(c.T, ada_w, ada_b.reshape(depth, 1, n3))


def _inproj_kernel(x_ref, sc_ref, sh_ref, w_ref, z_ref, h_scr, xs_scr):
    n = pl.program_id(2)
    tm, D = x_ref.shape

    @pl.when(n == 0)
    def _():
        h_scr[0] = (x_ref[...] * sc_ref[...] + sh_ref[...]).astype(BF16)
        for cc in range(D // LANES):
            cs = slice(cc * LANES, (cc + 1) * LANES)
            xs_scr[cc] = x_ref[:, cs]
        for g, d in enumerate(DILATIONS):
            if d == 1:
                continue
            rows = tm // d
            for cc in range(D // LANES):
                cs = slice(cc * LANES, (cc + 1) * LANES)
                sc = sc_ref[:, cs]
                sh = sh_ref[:, cs]
                for r in range(d):
                    xr = xs_scr[cc, pl.ds(r, rows, stride=d), :]
                    h_scr[g, r * rows:(r + 1) * rows, cs] = (xr * sc + sh).astype(BF16)

    is_gate = n == 3 * N_GROUPS
    g_sel = jnp.where(is_gate, 0, n // 3)
    z = jnp.dot(h_scr[g_sel], w_ref[...], preferred_element_type=F32)
    mult = jnp.where(jnp.logical_and(n % 3 == 0, jnp.logical_not(is_gate)),
                     HEAD_DIM ** -0.5 * LOG2E, 1.0).astype(F32)
    z = (z * mult).astype(BF16)
    for cc in range(z_ref.shape[0]):
        z_ref[cc] = z[:, cc * COL_W:(cc + 1) * COL_W]


def _attn_inproj(x, sc, sh, w_bf16):
    B, S, D = x.shape
    P = TOKEN_BLOCK
    nblk = S // P
    ncols = w_bf16.shape[1] // D
    ncw = D // COL_W
    return pl.pallas_call(
        _inproj_kernel,
        grid=(B, nblk, ncols),
        in_specs=[
            pl.BlockSpec((None, P, D), lambda b, i, n: (b, i, 0)),
            pl.BlockSpec((None, 1, D), lambda b, i, n: (b, 0, 0)),
            pl.BlockSpec((None, 1, D), lambda b, i, n: (b, 0, 0)),
            pl.BlockSpec((D, D), lambda b, i, n: (0, n)),
        ],
        out_specs=pl.BlockSpec((None, None, None, ncw, P, COL_W),
                               lambda b, i, n: (n, b, i, 0, 0, 0)),
        out_shape=jax.ShapeDtypeStruct((ncols, B, nblk, ncw, P, COL_W), BF16),
        scratch_shapes=[pltpu.VMEM((N_GROUPS, P, D), BF16),
                        pltpu.VMEM((D // LANES, P, LANES), F32)],
        compiler_params=pltpu.CompilerParams(
            dimension_semantics=("arbitrary", "arbitrary", "arbitrary"),
            vmem_limit_bytes=VMEM_LIMIT),
        name="attn_inproj",
    )(x, sc, sh, w_bf16)


def _attn_kernel(slopes_ref, *refs):
    P = TOKEN_BLOCK
    group_refs = [refs[7 * g:7 * g + 7] for g in range(N_GROUPS)]
    gate_ref = refs[7 * N_GROUPS]
    y_ref = refs[7 * N_GROUPS + 1]
    scr = refs[7 * N_GROUPS + 2:]
    kv_scr = [scr[2 * g:2 * g + 2] for g in range(N_GROUPS)]
    bias_scr, s_scr, p_scr, o_scr, l_scr = scr[2 * N_GROUPS:]

    i = pl.program_id(1)
    hq = pl.program_id(2)
    nblk = pl.num_programs(1)

    lane = lax.broadcasted_iota(jnp.int32, (1, COL_W), 1)
    head_masks = [(lane >= h * HEAD_DIM) & (lane < (h + 1) * HEAD_DIM)
                  for h in range(HEADS_PER_STEP)]
    kk = lax.broadcasted_iota(jnp.int32, (K_TILE, COL_W), 0)
    qq = lax.broadcasted_iota(jnp.int32, (K_TILE, COL_W), 1) % Q_CHUNK
    rel = jnp.abs(kk - HALF_WIN - qq)
    dist = rel.astype(F32)
    band = jnp.where(rel <= HALF_WIN, 0.0, NEG_INF).astype(F32)
    lo_pen = jnp.where(kk < HALF_WIN, NEG_INF, 0.0).astype(F32)
    hi_pen = jnp.where(kk >= K_TILE - HALF_WIN, NEG_INF, 0.0).astype(F32)
    pair_lo = lax.broadcasted_iota(jnp.int32, (1, LANES), 1) < HEAD_DIM
    nt_dims = (((1,), (1,)), ((), ()))
    tn_dims = (((0,), (0,)), ((), ()))
    n_units = P // Q_CHUNK
    n_steps = n_units // UNITS_PER_STEP
    assert n_steps % 2 == 0
    lse_rows = lax.broadcasted_iota(jnp.int32, (P_ROWS - K_TILE, LANES), 0) < 3
    rhs_tail = jnp.concatenate(
        [jnp.zeros((P_ROWS - K_TILE, LANES), BF16), jnp.where(lse_rows, 1.0, 0.0).astype(BF16)],
        axis=1)
    p_scr[:, :, K_TILE:, :] = jnp.zeros((2, UNITS_PER_STEP, P_ROWS - K_TILE, COL_W), BF16)

    for g, d in enumerate(DILATIONS):
        q_ref, k_ref, v_ref, kp_ref, kn_ref, vp_ref, vn_ref = group_refs[g]
        kext, vext = kv_scr[g]
        rows = P // d
        nc = rows // Q_CHUNK

        kext[:, 0:HALF_WIN, :] = kp_ref[...]
        kext[:, HALF_WIN:HALF_WIN + rows, :] = k_ref[...]
        kext[:, HALF_WIN + rows:, :] = kn_ref[...]
        vext[:, 0:HALF_WIN, :] = vp_ref[...]
        vext[:, HALF_WIN:HALF_WIN + rows, :] = v_ref[...]
        vext[:, HALF_WIN + rows:, :] = vn_ref[...]

        slope_lane = jnp.zeros((1, COL_W), F32)
        for h in range(HEADS_PER_STEP):
            slope = slopes_ref[g * HEADS + hq * HEADS_PER_STEP + h] * float(d)
            slope_lane = jnp.where(head_masks[h], slope, slope_lane)
        base = (band - slope_lane * dist) * LOG2E
        bias_scr[0] = base
        bias_scr[1] = base + lo_pen
        bias_scr[2] = base + hi_pen

        def coords(u, d=d, nc=nc):
            r, c = u // nc, u % nc
            off = c * Q_CHUNK
            tok0 = c * (Q_CHUNK * d) + r
            if not isinstance(u, int):
                off = pl.multiple_of(off, Q_CHUNK)
            if d == 1:
                dst = pl.ds(off, Q_CHUNK)
            else:
                dst = pl.ds(tok0, Q_CHUNK, stride=d)
            return r, c, off, dst

        def qk(u, s_buf, q_ref=q_ref, kext=kext, coords=coords):
            r, _, off, _ = coords(u)
            q = q_ref[r, pl.ds(off, Q_CHUNK), :]
            kt = kext[r, pl.ds(off, K_TILE), :]
            zero = jnp.zeros_like(q)
            qs = jnp.concatenate([jnp.where(hm, q, zero) for hm in head_masks], axis=0)
            s_buf[...] = lax.dot_general(kt, qs, nt_dims, preferred_element_type=F32)

        def softmax(u, s_buf, p_buf, nc=nc, coords=coords):
            _, c, _, _ = coords(u)
            var = jnp.where((i == 0) & (c == 0), 1,
                            jnp.where((i == nblk - 1) & (c == nc - 1), 2, 0))
            chunks = [slice(rc * SM_ROWS, (rc + 1) * SM_ROWS) for rc in range(K_TILE // SM_ROWS)]
            fold = (SM_ROWS // SUBLANES, SUBLANES, COL_W)
            m8 = None
            for rs in chunks:
                t = s_buf[rs, :] + bias_scr[var, rs, :]
                s_buf[rs, :] = t
                tm = jnp.max(t.reshape(fold), axis=0)
                m8 = tm if m8 is None else jnp.maximum(m8, tm)
            m = jnp.max(m8, axis=0, keepdims=True)
            d8 = None
            for rs in chunks:
                p = jnp.exp2(s_buf[rs, :] - m)
                s_buf[rs, :] = p
                ps = jnp.sum(p.reshape(fold), axis=0)
                d8 = ps if d8 is None else d8 + ps
            den = jnp.sum(d8, axis=0, keepdims=True)
            inv = 1.0 / den
            for rs in chunks:
                p_buf[rs, :] = (s_buf[rs, :] * inv).astype(BF16)
            lse = m + jnp.log(den) * LOG2E
            hi = lse.astype(BF16)
            rest = lse - hi.astype(F32)
            mid = rest.astype(BF16)
            lo = (rest - mid.astype(F32)).astype(BF16)
            p_buf[K_TILE:K_TILE + BF16_ROWS, :] = jnp.concatenate(
                [hi, mid, lo, jnp.zeros((BF16_ROWS - 3, COL_W), BF16)], axis=0)

        def pv(u, p_buf, g=g, vext=vext, coords=coords):
            r, _, off, dst = coords(u)
            vt = vext[r, pl.ds(off, K_TILE), :]
            for k in range(HEADS_PER_STEP // 2):
                cs = slice(k * LANES, (k + 1) * LANES)
                rhs = jnp.concatenate(
                    [jnp.concatenate([vt[:, cs], jnp.zeros((K_TILE, LANES), BF16)], axis=1),
                     rhs_tail], axis=0)
                out = lax.dot_general(p_buf[:, cs], rhs, tn_dims, preferred_element_type=F32)
                even, odd = out[0:Q_CHUNK], out[Q_CHUNK:2 * Q_CHUNK]
                o_scr[g, k, dst, :] = jnp.where(pair_lo, even[:, 0:LANES], odd[:, 0:LANES])
                l_scr[g, k, dst, :] = jnp.where(pair_lo, even[:, LANES:], odd[:, LANES:])

        def step(t, par, do_pv=True, do_sm=True, do_qk=True, qk=qk, softmax=softmax, pv=pv):
            for j in range(UNITS_PER_STEP):
                if do_pv:
                    pv(UNITS_PER_STEP * (t - 2) + j, p_scr.at[par, j])
                if do_qk:
                    qk(UNITS_PER_STEP * t + j, s_scr.at[par, j])
            for j in range(UNITS_PER_STEP):
                if do_sm:
                    softmax(UNITS_PER_STEP * (t - 1) + j, s_scr.at[1 - par, j], p_scr.at[1 - par, j])

        step(0, 0, do_pv=False, do_sm=False)
        step(1, 1, do_pv=False)

        for t in range(2, n_steps):
            step(t, t % 2)
        step(n_steps, 0, do_qk=False)
        step(n_steps + 1, 1, do_sm=False, do_qk=False)

    mrows = 128

    def merge(t, carry):
        sl = pl.ds(pl.multiple_of(t * mrows, mrows), mrows)
        for cc in range(COL_W // LANES):
            cs = slice(cc * LANES, (cc + 1) * LANES)
            l0, l1, l2 = l_scr[0, cc, sl, :], l_scr[1, cc, sl, :], l_scr[2, cc, sl, :]
            mx = jnp.maximum(jnp.maximum(l0, l1), l2)
            e0, e1, e2 = jnp.exp2(l0 - mx), jnp.exp2(l1 - mx), jnp.exp2(l2 - mx)
            inv = 1.0 / (e0 + e1 + e2)
            o = (e0 * o_scr[0, cc, sl, :] + e1 * o_scr[1, cc, sl, :]
                 + e2 * o_scr[2, cc, sl, :]) * inv
            gt = gate_ref[sl, cs].astype(F32)
            y_ref[sl, cs] = (o * _silu(gt)).astype(BF16)
        return carry

    lax.fori_loop(0, P // mrows, merge, 0)


def _attention(z, slopes, B, S, D):
    P = TOKEN_BLOCK
    nblk = S // P
    ncw = D // COL_W
    ncols = z.shape[0]
    in_specs = []
    operands = []
    scratch = []
    for g, d in enumerate(DILATIONS):
        rows = P // d
        nh = rows // HALF_WIN
        zg = z.reshape(ncols, B, nblk, ncw, d, rows, COL_W)
        zh = z.reshape(ncols, B, nblk, ncw, d, nh, HALF_WIN, COL_W)
        full = (None, None, None, None, d, rows, COL_W)
        halo = (None, None, None, None, d, None, HALF_WIN, COL_W)

        def cur(n):
            return pl.BlockSpec(full, lambda b, i, hq, s, n=n: (n, b, i, hq, 0, 0, 0))

        def prev(n, nh=nh):
            return pl.BlockSpec(halo, lambda b, i, hq, s, n=n, nh=nh:
                                (n, b, jnp.maximum(i - 1, 0), hq, 0, nh - 1, 0, 0))

        def nxt(n, nblk=nblk):
            return pl.BlockSpec(halo, lambda b, i, hq, s, n=n, nblk=nblk:
                                (n, b, jnp.minimum(i + 1, nblk - 1), hq, 0, 0, 0, 0))

        nq, nk, nv = 3 * g, 3 * g + 1, 3 * g + 2
        in_specs += [cur(nq), cur(nk), cur(nv), prev(nk), nxt(nk), prev(nv), nxt(nv)]
        operands += [zg, zg, zg, zh, zh, zh, zh]
        scratch += [pltpu.VMEM((d, rows + 2 * HALF_WIN, COL_W), BF16)] * 2
    zgate = z.reshape(ncols, B, nblk, ncw, P, COL_W)
    in_specs.append(pl.BlockSpec((None, None, None, None, P, COL_W),
                                 lambda b, i, hq, s: (ncols - 1, b, i, hq, 0, 0)))
    operands.append(zgate)
    scratch += [
        pltpu.VMEM((3, K_TILE, COL_W), F32),
        pltpu.VMEM((2, UNITS_PER_STEP, K_TILE, COL_W), F32),
        pltpu.VMEM((2, UNITS_PER_STEP, P_ROWS, COL_W), BF16),
        pltpu.VMEM((N_GROUPS, COL_W // LANES, P, LANES), F32),
        pltpu.VMEM((N_GROUPS, COL_W // LANES, P, LANES), F32),
    ]
    grid_spec = pltpu.PrefetchScalarGridSpec(
        num_scalar_prefetch=1,
        grid=(B, nblk, ncw),
        in_specs=in_specs,
        out_specs=pl.BlockSpec((None, P, COL_W), lambda b, i, hq, s: (b, i, hq)),
        scratch_shapes=scratch,
    )
    return pl.pallas_call(
        _attn_kernel,
        grid_spec=grid_spec,
        out_shape=jax.ShapeDtypeStruct((B, S, D), BF16),
        compiler_params=pltpu.CompilerParams(
            dimension_semantics=("arbitrary", "arbitrary", "arbitrary"),
            vmem_limit_bytes=VMEM_LIMIT),
        name="dilated_attn",
    )(slopes, *operands)


def _outproj_kernel(y_ref, x_ref, gc_ref, w_ref, lg_ref, lb_ref, o_ref):
    out = jnp.dot(y_ref[...], w_ref[...], preferred_element_type=F32)
    res = DEEPNORM_ALPHA * x_ref[...] + gc_ref[...] * out
    o_ref[...] = _layer_norm_rows(res, lg_ref[...], lb_ref[...])


def _outproj_ln(y, x, gate_c, w_bf16, ln_g, ln_b):
    B, S, D = x.shape
    K = y.shape[-1]
    tm = 512
    return pl.pallas_call(
        _outproj_kernel,
        grid=(B, S // tm),
        in_specs=[
            pl.BlockSpec((None, tm, K), lambda b, i: (b, i, 0)),
            pl.BlockSpec((None, tm, D), lambda b, i: (b, i, 0)),
            pl.BlockSpec((None, 1, D), lambda b, i: (b, 0, 0)),
            pl.BlockSpec((K, D), lambda b, i: (0, 0)),
            pl.BlockSpec((1, D), lambda b, i: (0, 0)),
            pl.BlockSpec((1, D), lambda b, i: (0, 0)),
        ],
        out_specs=pl.BlockSpec((None, tm, D), lambda b, i: (b, i, 0)),
        out_shape=jax.ShapeDtypeStruct((B, S, D), F32),
        compiler_params=pltpu.CompilerParams(
            dimension_semantics=("arbitrary", "arbitrary"),
            vmem_limit_bytes=VMEM_LIMIT),
        name="attn_outproj_ln",
    )(y, x, gate_c, w_bf16, ln_g, ln_b)


SGU_COLS = 256


def _gelu_exact(x):
    return 0.5 * x * (1.0 + lax.erf(x * (2.0 ** -0.5)))


def _sgu_kernel(x_ref, sc_ref, sh_ref, gc_ref, win_ref, lng_ref, lnb_ref, ws_ref, bs_ref,
                wout_ref, plg_ref, plb_ref, o_ref, h_scr, v_scr, y_scr):
    tm = x_ref.shape[0]
    E = v_scr.shape[1]
    x = x_ref[...]
    h_scr[...] = (x * sc_ref[...] + sh_ref[...]).astype(BF16)

    zv = jnp.dot(h_scr[...], win_ref[:, E:2 * E], preferred_element_type=F32)
    v = _layer_norm_rows(_gelu_exact(zv), lng_ref[...], lnb_ref[...])
    v_scr[...] = v.astype(BF16)

    gpb = SGU_COLS // SGU_GROUP_CH
    for cb in range(E // SGU_COLS):
        c0 = cb * SGU_COLS
        zu = jnp.dot(h_scr[...], win_ref[:, c0:c0 + SGU_COLS], preferred_element_type=F32)
        zg = jnp.dot(h_scr[...], win_ref[:, 2 * E + c0:2 * E + c0 + SGU_COLS],
                     preferred_element_type=F32)
        ug = _gelu_exact(zu) * _silu(zg)
        for gi in range(gpb):
            g = cb * gpb + gi
            l0 = gi * SGU_GROUP_CH
            wsg = ws_ref[g]
            bsg = bs_ref[:, g:g + 1]
            gcols = slice(c0 + l0, c0 + l0 + SGU_GROUP_CH)
            for nn in range(0, tm // SGU_CHUNK, 2):
                ra = slice(nn * SGU_CHUNK, (nn + 1) * SGU_CHUNK)
                rb = slice((nn + 1) * SGU_CHUNK, (nn + 2) * SGU_CHUNK)
                vc = jnp.concatenate([v_scr[ra, gcols], v_scr[rb, gcols]], axis=1)
                sv = jnp.dot(wsg, vc, preferred_element_type=F32) + bsg
                for rr, half in ((ra, 0), (rb, 1)):
                    y = (ug[rr, l0:l0 + SGU_GROUP_CH]
                         * sv[:, half * SGU_GROUP_CH:(half + 1) * SGU_GROUP_CH])
                    y_scr[rr, gcols] = y.astype(BF16)

    out = jnp.dot(y_scr[...], wout_ref[...], preferred_element_type=F32)
    res = DEEPNORM_ALPHA * x + gc_ref[...] * out
    o_ref[...] = _layer_norm_rows(res, plg_ref[...], plb_ref[...])


def _sgu_layer(x, sc, sh, gate_c, w_in, ln_g, ln_b, w_s, b_s_t, w_out, post_g, post_b):
    B, S, D = x.shape
    E = w_out.shape[0]
    G = w_s.shape[0]
    tm = 512
    const2 = lambda b, i: (0, 0)
    single = pl.Buffered(1)
    return pl.pallas_call(
        _sgu_kernel,
        grid=(B, S // tm),
        in_specs=[
            pl.BlockSpec((None, tm, D), lambda b, i: (b, i, 0)),
            pl.BlockSpec((None, 1, D), lambda b, i: (b, 0, 0)),
            pl.BlockSpec((None, 1, D), lambda b, i: (b, 0, 0)),
            pl.BlockSpec((None, 1, D), lambda b, i: (b, 0, 0)),
            pl.BlockSpec((D, 3 * E), const2, pipeline_mode=single),
            pl.BlockSpec((1, E), const2),
            pl.BlockSpec((1, E), const2),
            pl.BlockSpec((G, SGU_CHUNK, SGU_CHUNK), lambda b, i: (0, 0, 0), pipeline_mode=single),
            pl.BlockSpec((SGU_CHUNK, G), const2),
            pl.BlockSpec((E, D), const2, pipeline_mode=single),
            pl.BlockSpec((1, D), const2),
            pl.BlockSpec((1, D), const2),
        ],
        out_specs=pl.BlockSpec((None, tm, D), lambda b, i: (b, i, 0)),
        out_shape=jax.ShapeDtypeStruct((B, S, D), F32),
        scratch_shapes=[
            pltpu.VMEM((tm, D), BF16),
            pltpu.VMEM((tm, E), BF16),
            pltpu.VMEM((tm, E), BF16),
        ],
        compiler_params=pltpu.CompilerParams(
            dimension_semantics=("arbitrary", "arbitrary"),
            vmem_limit_bytes=VMEM_LIMIT),
        name="sgu_layer",
    )(x, sc, sh, gate_c, w_in, ln_g, ln_b, w_s, b_s_t, w_out, post_g, post_b)


def _alibi_slopes(n):
    return 2.0 ** (-8.0 * jnp.arange(1, n + 1, dtype=F32) / n)


def kernel(x, c, ada_w, ada_b, post_ln_g, post_ln_b, a_w_in, a_w_out, b_w_in, b_ln_g, b_ln_b,
           b_w_s, b_b_s, b_w_out):
    B, S, D = x.shape
    mod = _modulation(c, ada_w, ada_b)
    slopes = _alibi_slopes(N_GROUPS * HEADS)

    def mod_parts(i):
        shift = mod[i, :, None, 0:D]
        scale1 = 1.0 + mod[i, :, None, D:2 * D]
        gate_c = mod[i, :, None, 2 * D:3 * D]
        return scale1, shift, gate_c

    sc, sh, gc = mod_parts(0)
    z = _attn_inproj(x, sc, sh, a_w_in[0].astype(BF16))
    y = _attention(z, slopes, B, S, D)
    x = _outproj_ln(y, x, gc, a_w_out[0].astype(BF16),
                    post_ln_g[0][None, :], post_ln_b[0][None, :])

    sc, sh, gc = mod_parts(1)
    x = _sgu_layer(x, sc, sh, gc, b_w_in[0].astype(BF16), b_ln_g[0][None, :], b_ln_b[0][None, :],
                   b_w_s[0].astype(BF16), b_b_s[0].T, b_w_out[0].astype(BF16),
                   post_ln_g[1][None, :], post_ln_b[1][None, :])
    return x
```

```python
import functools

import numpy as np
import jax
import jax.numpy as jnp
from jax import lax
from jax.experimental import pallas as pl
from jax.experimental.pallas import tpu as pltpu

F32 = jnp.float32
BF16 = jnp.bfloat16

DEPTH = 2
ATTN_GROUPS = ((128, 1), (512, 4), (2048, 16))
N_GROUPS = len(ATTN_GROUPS)
DILATIONS = tuple(d for _, d in ATTN_GROUPS)
HEADS = 16
HEAD_DIM = 64
HALF_WIN = 64
SGU_CHUNK = 128
SGU_GROUP_CH = 128
DEEPNORM_ALPHA = (2 * DEPTH) ** 0.25
LN_EPS = 1e-5
NEG_INF = -1e30
LOG2E = 1.4426950408889634
LANES = 128

TOKEN_BLOCK = 1024
HEADS_PER_STEP = 4
COL_W = HEADS_PER_STEP * HEAD_DIM
Q_CHUNK = 64
K_TILE = Q_CHUNK + 2 * HALF_WIN
UNITS_PER_STEP = 4
BF16_ROWS = 16
SUBLANES = 8
SM_ROWS = 32
P_ROWS = 256
VMEM_LIMIT = 56 * 1024 * 1024

assert all(w // (2 * d) == HALF_WIN for w, d in ATTN_GROUPS)


def _silu(x):
    return x * jax.nn.sigmoid(x)


def _layer_norm_rows(x, g, b):
    mu = jnp.mean(x, axis=-1, keepdims=True)
    xc = x - mu
    var = jnp.mean(xc * xc, axis=-1, keepdims=True)
    return xc * lax.rsqrt(var + LN_EPS) * g + b


def _mod_kernel(ct_ref, w_ref, b_ref, o_ref):
    ct = ct_ref[...]
    cond = _silu(ct)
    w = w_ref[...]
    for b in range(ct.shape[1]):
        col = cond[:, b:b + 1]
        o_ref[b:b + 1, :] = jnp.sum(col * w, axis=0, keepdims=True) + b_ref[...]


def _modulation(c, ada_w, ada_b):
    B, D = c.shape
    depth, _, n3 = ada_w.shape
    tn = 512
    return pl.pallas_call(
        _mod_kernel,
        grid=(depth, n3 // tn),
        in_specs=[
            pl.BlockSpec((D, B), lambda i, n: (0, 0)),
            pl.BlockSpec((None, D, tn), lambda i, n: (i, 0, n)),
            pl.BlockSpec((None, 1, tn), lambda i, n: (i, 0, n)),
        ],
        out_specs=pl.BlockSpec((None, B, tn), lambda i, n: (i, 0, n)),
        out_shape=jax.ShapeDtypeStruct((depth, B, n3), F32),
        compiler_params=pltpu.CompilerParams(
            dimension_semantics=("arbitrary", "arbitrary")),
        name="adaln_mod",
    )(c.T, ada_w, ada_b.reshape(depth, 1, n3))


def _inproj_kernel(x_ref, sc_ref, sh_ref, w_ref, z_ref, h_scr, xs_scr):
    n = pl.program_id(2)
    tm, D = x_ref.shape

    @pl.when(n == 0)
    def _():
        for cc in range(D // LANES):
            cs = slice(cc * LANES, (cc + 1) * LANES)
            h = x_ref[:, cs] * sc_ref[:, cs] + sh_ref[:, cs]
            xs_scr[cc] = h
            h_scr[0, :, cs] = h.astype(BF16)
        for g, d in enumerate(DILATIONS):
            if d == 1:
                continue
            rows = tm // d
            for cc in range(D // LANES):
                cs = slice(cc * LANES, (cc + 1) * LANES)
                for r in range(d):
                    hr = xs_scr[cc, pl.ds(r, rows, stride=d), :]
                    h_scr[g, r * rows:(r + 1) * rows, cs] = hr.astype(BF16)

    is_gate = n == 3 * N_GROUPS
    g_sel = jnp.where(is_gate, 0, n // 3)
    z = jnp.dot(h_scr[g_sel], w_ref[...], preferred_element_type=F32)
    mult = jnp.where(jnp.logical_and(n % 3 == 0, jnp.logical_not(is_gate)),
                     HEAD_DIM ** -0.5 * LOG2E, 1.0).astype(F32)
    z = (z * mult).astype(BF16)
    for cc in range(z_ref.shape[0]):
        z_ref[cc] = z[:, cc * COL_W:(cc + 1) * COL_W]


def _attn_inproj(x, sc, sh, w_bf16):
    B, S, D = x.shape
    P = TOKEN_BLOCK
    nblk = S // P
    ncols = w_bf16.shape[1] // D
    ncw = D // COL_W
    return pl.pallas_call(
        _inproj_kernel,
        grid=(B, nblk, ncols),
        in_specs=[
            pl.BlockSpec((None, P, D), lambda b, i, n: (b, i, 0)),
            pl.BlockSpec((None, 1, D), lambda b, i, n: (b, 0, 0)),
            pl.BlockSpec((None, 1, D), lambda b, i, n: (b, 0, 0)),
            pl.BlockSpec((D, D), lambda b, i, n: (0, n)),
        ],
        out_specs=pl.BlockSpec((None, None, None, ncw, P, COL_W),
                               lambda b, i, n: (n, b, i, 0, 0, 0)),
        out_shape=jax.ShapeDtypeStruct((ncols, B, nblk, ncw, P, COL_W), BF16),
        scratch_shapes=[pltpu.VMEM((N_GROUPS, P, D), BF16),
                        pltpu.VMEM((D // LANES, P, LANES), F32)],
        compiler_params=pltpu.CompilerParams(
            dimension_semantics=("arbitrary", "arbitrary", "arbitrary"),
            vmem_limit_bytes=VMEM_LIMIT),
        name="attn_inproj",
    )(x, sc, sh, w_bf16)


def _attn_kernel(slopes_ref, *refs):
    P = TOKEN_BLOCK
    group_refs = [refs[7 * g:7 * g + 7] for g in range(N_GROUPS)]
    gate_ref = refs[7 * N_GROUPS]
    y_ref = refs[7 * N_GROUPS + 1]
    scr = refs[7 * N_GROUPS + 2:]
    kv_scr = [scr[2 * g:2 * g + 2] for g in range(N_GROUPS)]
    bias_scr, s_scr, p_scr, o_scr, l_scr = scr[2 * N_GROUPS:]

    i = pl.program_id(1)
    hq = pl.program_id(2)
    nblk = pl.num_programs(1)

    lane = lax.broadcasted_iota(jnp.int32, (1, COL_W), 1)
    head_masks = [(lane >= h * HEAD_DIM) & (lane < (h + 1) * HEAD_DIM)
                  for h in range(HEADS_PER_STEP)]
    kk = lax.broadcasted_iota(jnp.int32, (K_TILE, COL_W), 0)
    qq = lax.broadcasted_iota(jnp.int32, (K_TILE, COL_W), 1) % Q_CHUNK
    rel = jnp.abs(kk - HALF_WIN - qq)
    dist = rel.astype(F32)
    band = jnp.where(rel <= HALF_WIN, 0.0, NEG_INF).astype(F32)
    lo_pen = jnp.where(kk < HALF_WIN, NEG_INF, 0.0).astype(F32)
    hi_pen = jnp.where(kk >= K_TILE - HALF_WIN, NEG_INF, 0.0).astype(F32)
    pair_lo = lax.broadcasted_iota(jnp.int32, (1, LANES), 1) < HEAD_DIM
    nt_dims = (((1,), (1,)), ((), ()))
    tn_dims = (((0,), (0,)), ((), ()))
    n_units = P // Q_CHUNK
    n_steps = n_units // UNITS_PER_STEP
    assert n_steps % 2 == 0
    lse_rows = lax.broadcasted_iota(jnp.int32, (P_ROWS - K_TILE, LANES), 0) < 3
    rhs_tail = jnp.concatenate(
        [jnp.zeros((P_ROWS - K_TILE, LANES), BF16), jnp.where(lse_rows, 1.0, 0.0).astype(BF16)],
        axis=1)
    p_scr[:, :, K_TILE:, :] = jnp.zeros((2, UNITS_PER_STEP, P_ROWS - K_TILE, COL_W), BF16)

    for g, d in enumerate(DILATIONS):
        q_ref, k_ref, v_ref, kp_ref, kn_ref, vp_ref, vn_ref = group_refs[g]
        kext, vext = kv_scr[g]
        rows = P // d
        nc = rows // Q_CHUNK

        kext[:, 0:HALF_WIN, :] = kp_ref[...]
        kext[:, HALF_WIN:HALF_WIN + rows, :] = k_ref[...]
        kext[:, HALF_WIN + rows:, :] = kn_ref[...]
        vext[:, 0:HALF_WIN, :] = vp_ref[...]
        vext[:, HALF_WIN:HALF_WIN + rows, :] = v_ref[...]
        vext[:, HALF_WIN + rows:, :] = vn_ref[...]

        slope_lane = jnp.zeros((1, COL_W), F32)
        for h in range(HEADS_PER_STEP):
            slope = slopes_ref[g * HEADS + hq * HEADS_PER_STEP + h] * float(d)
            slope_lane = jnp.where(head_masks[h], slope, slope_lane)
        base = (band - slope_lane * dist) * LOG2E
        bias_scr[0] = base
        bias_scr[1] = base + lo_pen
        bias_scr[2] = base + hi_pen

        def coords(u, d=d, nc=nc):
            r, c = u // nc, u % nc
            off = c * Q_CHUNK
            tok0 = c * (Q_CHUNK * d) + r
            if not isinstance(u, int):
                off = pl.multiple_of(off, Q_CHUNK)
            if d == 1:
                dst = pl.ds(off, Q_CHUNK)
            else:
                dst = pl.ds(tok0, Q_CHUNK, stride=d)
            return r, c, off, dst

        def qk(u, s_buf, q_ref=q_ref, kext=kext, coords=coords, nc=nc):
            r, _, off, _ = coords(u)
            q = q_ref[r, pl.ds(off, Q_CHUNK), :]
            kt = kext[r, pl.ds(off, K_TILE), :]
            zero = jnp.zeros_like(q)
            qs = jnp.concatenate([jnp.where(hm, q, zero) for hm in head_masks], axis=0)
            _, c, _, _ = coords(u)
            var = jnp.where((i == 0) & (c == 0), 1,
                            jnp.where((i == nblk - 1) & (c == nc - 1), 2, 0))
            s_buf[...] = (lax.dot_general(kt, qs, nt_dims, preferred_element_type=F32)
                          + bias_scr[var])

        def softmax(u, s_buf, p_buf):
            chunks = [slice(rc * SM_ROWS, (rc + 1) * SM_ROWS) for rc in range(K_TILE // SM_ROWS)]
            fold = (SM_ROWS // SUBLANES, SUBLANES, COL_W)
            m8 = None
            for rs in chunks:
                tm = jnp.max(s_buf[rs, :].reshape(fold), axis=0)
                m8 = tm if m8 is None else jnp.maximum(m8, tm)
            m = jnp.max(m8, axis=0, keepdims=True)
            d8 = None
            for rs in chunks:
                p = jnp.exp2(s_buf[rs, :] - m)
                s_buf[rs, :] = p
                ps = jnp.sum(p.reshape(fold), axis=0)
                d8 = ps if d8 is None else d8 + ps
            den = jnp.sum(d8, axis=0, keepdims=True)
            inv = 1.0 / den
            for rs in chunks:
                p_buf[rs, :] = (s_buf[rs, :] * inv).astype(BF16)
            lse = m + jnp.log(den) * LOG2E
            hi = lse.astype(BF16)
            rest = lse - hi.astype(F32)
            mid = rest.astype(BF16)
            lo = (rest - mid.astype(F32)).astype(BF16)
            p_buf[K_TILE:K_TILE + BF16_ROWS, :] = jnp.concatenate(
                [hi, mid, lo, jnp.zeros((BF16_ROWS - 3, COL_W), BF16)], axis=0)

        def pv(u, p_buf, g=g, vext=vext, coords=coords):
            r, _, off, dst = coords(u)
            vt = vext[r, pl.ds(off, K_TILE), :]
            for k in range(HEADS_PER_STEP // 2):
                cs = slice(k * LANES, (k + 1) * LANES)
                rhs = jnp.concatenate(
                    [jnp.concatenate([vt[:, cs], jnp.zeros((K_TILE, LANES), BF16)], axis=1),
                     rhs_tail], axis=0)
                out = lax.dot_general(p_buf[:, cs], rhs, tn_dims, preferred_element_type=F32)
                even, odd = out[0:Q_CHUNK], out[Q_CHUNK:2 * Q_CHUNK]
                o_scr[g, k, dst, :] = jnp.where(pair_lo, even[:, 0:LANES], odd[:, 0:LANES])
                l_scr[g, k, dst, :] = jnp.where(pair_lo, even[:, LANES:], odd[:, LANES:])

        def step(t, par, do_pv=True, do_sm=True, do_qk=True, qk=qk, softmax=softmax, pv=pv):
            for j in range(UNITS_PER_STEP):
                if do_pv:
                    pv(UNITS_PER_STEP * (t - 2) + j, p_scr.at[par, j])
                if do_qk:
                    qk(UNITS_PER_STEP * t + j, s_scr.at[par, j])
            for j in range(UNITS_PER_STEP):
                if do_sm:
                    softmax(UNITS_PER_STEP * (t - 1) + j, s_scr.at[1 - par, j], p_scr.at[1 - par, j])

        step(0, 0, do_pv=False, do_sm=False)
        step(1, 1, do_pv=False)

        for t in range(2, n_steps):
            step(t, t % 2)
        step(n_steps, 0, do_qk=False)
        step(n_steps + 1, 1, do_sm=False, do_qk=False)

    mrows = 128

    def merge(t, carry):
        sl = pl.ds(pl.multiple_of(t * mrows, mrows), mrows)
        for cc in range(COL_W // LANES):
            cs = slice(cc * LANES, (cc + 1) * LANES)
            l0, l1, l2 = l_scr[0, cc, sl, :], l_scr[1, cc, sl, :], l_scr[2, cc, sl, :]
            mx = jnp.maximum(jnp.maximum(l0, l1), l2)
            e0, e1, e2 = jnp.exp2(l0 - mx), jnp.exp2(l1 - mx), jnp.exp2(l2 - mx)
            inv = 1.0 / (e0 + e1 + e2)
            o = (e0 * o_scr[0, cc, sl, :] + e1 * o_scr[1, cc, sl, :]
                 + e2 * o_scr[2, cc, sl, :]) * inv
            gt = gate_ref[sl, cs].astype(F32)
            y_ref[sl, cs] = (o * _silu(gt)).astype(BF16)
        return carry

    lax.fori_loop(0, P // mrows, merge, 0)


def _attention(z, slopes, B, S, D):
    P = TOKEN_BLOCK
    nblk = S // P
    ncw = D // COL_W
    ncols = z.shape[0]
    in_specs = []
    operands = []
    scratch = []
    for g, d in enumerate(DILATIONS):
        rows = P // d
        nh = rows // HALF_WIN
        zg = z.reshape(ncols, B, nblk, ncw, d, rows, COL_W)
        zh = z.reshape(ncols, B, nblk, ncw, d, nh, HALF_WIN, COL_W)
        full = (None, None, None, None, d, rows, COL_W)
        halo = (None, None, None, None, d, None, HALF_WIN, COL_W)

        def cur(n):
            return pl.BlockSpec(full, lambda b, i, hq, s, n=n: (n, b, i, hq, 0, 0, 0))

        def prev(n, nh=nh):
            return pl.BlockSpec(halo, lambda b, i, hq, s, n=n, nh=nh:
                                (n, b, jnp.maximum(i - 1, 0), hq, 0, nh - 1, 0, 0))

        def nxt(n, nblk=nblk):
            return pl.BlockSpec(halo, lambda b, i, hq, s, n=n, nblk=nblk:
                                (n, b, jnp.minimum(i + 1, nblk - 1), hq, 0, 0, 0, 0))

        nq, nk, nv = 3 * g, 3 * g + 1, 3 * g + 2
        in_specs += [cur(nq), cur(nk), cur(nv), prev(nk), nxt(nk), prev(nv), nxt(nv)]
        operands += [zg, zg, zg, zh, zh, zh, zh]
        scratch += [pltpu.VMEM((d, rows + 2 * HALF_WIN, COL_W), BF16)] * 2
    zgate = z.reshape(ncols, B, nblk, ncw, P, COL_W)
    in_specs.append(pl.BlockSpec((None, None, None, None, P, COL_W),
                                 lambda b, i, hq, s: (ncols - 1, b, i, hq, 0, 0)))
    operands.append(zgate)
    scratch += [
        pltpu.VMEM((3, K_TILE, COL_W), F32),
        pltpu.VMEM((2, UNITS_PER_STEP, K_TILE, COL_W), F32),
        pltpu.VMEM((2, UNITS_PER_STEP, P_ROWS, COL_W), BF16),
        pltpu.VMEM((N_GROUPS, COL_W // LANES, P, LANES), F32),
        pltpu.VMEM((N_GROUPS, COL_W // LANES, P, LANES), F32),
    ]
    grid_spec = pltpu.PrefetchScalarGridSpec(
        num_scalar_prefetch=1,
        grid=(B, nblk, ncw),
        in_specs=in_specs,
        out_specs=pl.BlockSpec((None, P, COL_W), lambda b, i, hq, s: (b, i, hq)),
        scratch_shapes=scratch,
    )
    return pl.pallas_call(
        _attn_kernel,
        grid_spec=grid_spec,
        out_shape=jax.ShapeDtypeStruct((B, S, D), BF16),
        compiler_params=pltpu.CompilerParams(
            dimension_semantics=("arbitrary", "arbitrary", "arbitrary"),
            vmem_limit_bytes=VMEM_LIMIT),
        name="dilated_attn",
    )(slopes, *operands)


SGU_COLS = 256


def _gelu_exact(x):
    return 0.5 * x * (1.0 + lax.erf(x * (2.0 ** -0.5)))


def _sgu_kernel(ya_ref, x_ref, gca_ref, wa_ref, lga_ref, lba_ref,
                sc_ref, sh_ref, gc_ref, win_ref, lng_ref, lnb_ref, ws_ref, bs_ref,
                wout_ref, plg_ref, plb_ref, o_ref, x1_scr, h_scr, v_scr, y_scr):
    tm = x_ref.shape[0]
    E = v_scr.shape[1]

    out0 = jnp.dot(ya_ref[...], wa_ref[...], preferred_element_type=F32)
    res0 = DEEPNORM_ALPHA * x_ref[...] + gca_ref[...] * out0
    x1_scr[...] = _layer_norm_rows(res0, lga_ref[...], lba_ref[...])

    h_scr[...] = (x1_scr[...] * sc_ref[...] + sh_ref[...]).astype(BF16)

    zv = jnp.dot(h_scr[...], win_ref[:, E:2 * E], preferred_element_type=F32)
    v = _layer_norm_rows(_gelu_exact(zv), lng_ref[...], lnb_ref[...])
    v_scr[...] = v.astype(BF16)

    gpb = SGU_COLS // SGU_GROUP_CH
    for cb in range(E // SGU_COLS):
        c0 = cb * SGU_COLS
        zu = jnp.dot(h_scr[...], win_ref[:, c0:c0 + SGU_COLS], preferred_element_type=F32)
        zg = jnp.dot(h_scr[...], win_ref[:, 2 * E + c0:2 * E + c0 + SGU_COLS],
                     preferred_element_type=F32)
        ug = _gelu_exact(zu) * _silu(zg)
        for gi in range(gpb):
            g = cb * gpb + gi
            l0 = gi * SGU_GROUP_CH
            wsg = ws_ref[g]
            bsg = bs_ref[:, g:g + 1]
            gcols = slice(c0 + l0, c0 + l0 + SGU_GROUP_CH)
            for nn in range(0, tm // SGU_CHUNK, 2):
                ra = slice(nn * SGU_CHUNK, (nn + 1) * SGU_CHUNK)
                rb = slice((nn + 1) * SGU_CHUNK, (nn + 2) * SGU_CHUNK)
                vc = jnp.concatenate([v_scr[ra, gcols], v_scr[rb, gcols]], axis=1)
                sv = jnp.dot(wsg, vc, preferred_element_type=F32) + bsg
                for rr, half in ((ra, 0), (rb, 1)):
                    y = (ug[rr, l0:l0 + SGU_GROUP_CH]
                         * sv[:, half * SGU_GROUP_CH:(half + 1) * SGU_GROUP_CH])
                    y_scr[rr, gcols] = y.astype(BF16)

    out = jnp.dot(y_scr[...], wout_ref[...], preferred_element_type=F32)
    res = DEEPNORM_ALPHA * x1_scr[...] + gc_ref[...] * out
    o_ref[...] = _layer_norm_rows(res, plg_ref[...], plb_ref[...])


def _outproj_sgu_layer(y_a, x, gate_c_a, w_out_a, post_g_a, post_b_a,
                       sc, sh, gate_c, w_in, ln_g, ln_b, w_s, b_s_t, w_out, post_g, post_b):
    B, S, D = x.shape
    E = w_out.shape[0]
    G = w_s.shape[0]
    tm = 512
    const2 = lambda b, i: (0, 0)
    single = pl.Buffered(1)
    return pl.pallas_call(
        _sgu_kernel,
        grid=(B, S // tm),
        in_specs=[
            pl.BlockSpec((None, tm, D), lambda b, i: (b, i, 0)),
            pl.BlockSpec((None, tm, D), lambda b, i: (b, i, 0)),
            pl.BlockSpec((None, 1, D), lambda b, i: (b, 0, 0)),
            pl.BlockSpec((D, D), const2, pipeline_mode=single),
            pl.BlockSpec((1, D), const2),
            pl.BlockSpec((1, D), const2),
            pl.BlockSpec((None, 1, D), lambda b, i: (b, 0, 0)),
            pl.BlockSpec((None, 1, D), lambda b, i: (b, 0, 0)),
            pl.BlockSpec((None, 1, D), lambda b, i: (b, 0, 0)),
            pl.BlockSpec((D, 3 * E), const2, pipeline_mode=single),
            pl.BlockSpec((1, E), const2),
            pl.BlockSpec((1, E), const2),
            pl.BlockSpec((G, SGU_CHUNK, SGU_CHUNK), lambda b, i: (0, 0, 0), pipeline_mode=single),
            pl.BlockSpec((SGU_CHUNK, G), const2),
            pl.BlockSpec((E, D), const2, pipeline_mode=single),
            pl.BlockSpec((1, D), const2),
            pl.BlockSpec((1, D), const2),
        ],
        out_specs=pl.BlockSpec((None, tm, D), lambda b, i: (b, i, 0)),
        out_shape=jax.ShapeDtypeStruct((B, S, D), F32),
        scratch_shapes=[
            pltpu.VMEM((tm, D), F32),
            pltpu.VMEM((tm, D), BF16),
            pltpu.VMEM((tm, E), BF16),
            pltpu.VMEM((tm, E), BF16),
        ],
        compiler_params=pltpu.CompilerParams(
            dimension_semantics=("arbitrary", "arbitrary"),
            vmem_limit_bytes=VMEM_LIMIT),
        name="outproj_sgu_layer",
    )(y_a, x, gate_c_a, w_out_a, post_g_a, post_b_a,
      sc, sh, gate_c, w_in, ln_g, ln_b, w_s, b_s_t, w_out, post_g, post_b)


def _alibi_slopes(n):
    return 2.0 ** (-8.0 * jnp.arange(1, n + 1, dtype=F32) / n)


def kernel(x, c, ada_w, ada_b, post_ln_g, post_ln_b, a_w_in, a_w_out, b_w_in, b_ln_g, b_ln_b,
           b_w_s, b_b_s, b_w_out):
    B, S, D = x.shape
    mod = _modulation(c, ada_w, ada_b)
    slopes = _alibi_slopes(N_GROUPS * HEADS)

    def mod_parts(i):
        shift = mod[i, :, None, 0:D]
        scale1 = 1.0 + mod[i, :, None, D:2 * D]
        gate_c = mod[i, :, None, 2 * D:3 * D]
        return scale1, shift, gate_c

    sc, sh, gc_a = mod_parts(0)
    z = _attn_inproj(x, sc, sh, a_w_in[0].astype(BF16))
    y = _attention(z, slopes, B, S, D)

    sc, sh, gc = mod_parts(1)
    return _outproj_sgu_layer(
        y, x, gc_a, a_w_out[0].astype(BF16), post_ln_g[0][None, :], post_ln_b[0][None, :],
        sc, sh, gc, b_w_in[0].astype(BF16), b_ln_g[0][None, :], b_ln_b[0][None, :],
        b_w_s[0].astype(BF16), b_b_s[0].T, b_w_out[0].astype(BF16),
        post_ln_g[1][None, :], post_ln_b[1][None, :])
```

```python
import functools

import numpy as np
import jax
import jax.numpy as jnp
from jax import lax
from jax.experimental import pallas as pl
from jax.experimental.pallas import tpu as pltpu

F32 = jnp.float32
BF16 = jnp.bfloat16

DEPTH = 2
ATTN_GROUPS = ((128, 1), (512, 4), (2048, 16))
N_GROUPS = len(ATTN_GROUPS)
DILATIONS = tuple(d for _, d in ATTN_GROUPS)
HEADS = 16
HEAD_DIM = 64
HALF_WIN = 64
SGU_CHUNK = 128
SGU_GROUP_CH = 128
DEEPNORM_ALPHA = (2 * DEPTH) ** 0.25
LN_EPS = 1e-5
NEG_INF = -1e30
LOG2E = 1.4426950408889634
LANES = 128

TOKEN_BLOCK = 1024
HEADS_PER_STEP = 4
COL_W = HEADS_PER_STEP * HEAD_DIM
Q_CHUNK = 64
K_TILE = Q_CHUNK + 2 * HALF_WIN
INPROJ_COLS_PER_STEP = 2
UNITS_PER_STEP = 4
BF16_ROWS = 16
SUBLANES = 8
SM_ROWS = 32
P_ROWS = 256
VMEM_LIMIT = 56 * 1024 * 1024

assert all(w // (2 * d) == HALF_WIN for w, d in ATTN_GROUPS)


def _silu(x):
    return x * jax.nn.sigmoid(x)


def _layer_norm_rows(x, g, b):
    mu = jnp.mean(x, axis=-1, keepdims=True)
    xc = x - mu
    var = jnp.mean(xc * xc, axis=-1, keepdims=True)
    return xc * lax.rsqrt(var + LN_EPS) * g + b


def _mod_kernel(ct_ref, w_ref, b_ref, o_ref):
    ct = ct_ref[...]
    cond = _silu(ct)
    w = w_ref[...]
    for b in range(ct.shape[1]):
        col = cond[:, b:b + 1]
        o_ref[b:b + 1, :] = jnp.sum(col * w, axis=0, keepdims=True) + b_ref[...]


def _modulation(c, ada_w, ada_b):
    B, D = c.shape
    depth, _, n3 = ada_w.shape
    tn = 512
    return pl.pallas_call(
        _mod_kernel,
        grid=(depth, n3 // tn),
        in_specs=[
            pl.BlockSpec((D, B), lambda i, n: (0, 0)),
            pl.BlockSpec((None, D, tn), lambda i, n: (i, 0, n)),
            pl.BlockSpec((None, 1, tn), lambda i, n: (i, 0, n)),
        ],
        out_specs=pl.BlockSpec((None, B, tn), lambda i, n: (i, 0, n)),
        out_shape=jax.ShapeDtypeStruct((depth, B, n3), F32),
        compiler_params=pltpu.CompilerParams(
            dimension_semantics=("arbitrary", "arbitrary")),
        name="adaln_mod",
    )(c.T, ada_w, ada_b.reshape(depth, 1, n3))


def _inproj_kernel(x_ref, sc_ref, sh_ref, w_ref, z_ref, h_scr, xs_scr):
    step = pl.program_id(2)
    tm, D = x_ref.shape

    @pl.when(step == 0)
    def _():
        for cc in range(D // LANES):
            cs = slice(cc * LANES, (cc + 1) * LANES)
            h = x_ref[:, cs] * sc_ref[:, cs] + sh_ref[:, cs]
            xs_scr[cc] = h
            h_scr[0, :, cs] = h.astype(BF16)
        for g, d in enumerate(DILATIONS):
            if d == 1:
                continue
            rows = tm // d
            for cc in range(D // LANES):
                cs = slice(cc * LANES, (cc + 1) * LANES)
                for r in range(d):
                    hr = xs_scr[cc, pl.ds(r, rows, stride=d), :]
                    h_scr[g, r * rows:(r + 1) * rows, cs] = hr.astype(BF16)

    for j in range(z_ref.shape[0]):
        n = step * z_ref.shape[0] + j
        is_gate = n == 3 * N_GROUPS
        g_sel = jnp.where(is_gate, 0, n // 3)
        w = w_ref[:, pl.ds(pl.multiple_of(n * D, D), D)]
        z = jnp.dot(h_scr[g_sel], w, preferred_element_type=F32)
        mult = jnp.where(jnp.logical_and(n % 3 == 0, jnp.logical_not(is_gate)),
                         HEAD_DIM ** -0.5 * LOG2E, 1.0).astype(F32)
        z = (z * mult).astype(BF16)
        for cc in range(z_ref.shape[1]):
            z_ref[j, cc] = z[:, cc * COL_W:(cc + 1) * COL_W]


def _attn_inproj(x, sc, sh, w_bf16):
    B, S, D = x.shape
    P = TOKEN_BLOCK
    nblk = S // P
    ncols = w_bf16.shape[1] // D
    ncw = D // COL_W
    return pl.pallas_call(
        _inproj_kernel,
        grid=(B, nblk, ncols // INPROJ_COLS_PER_STEP),
        in_specs=[
            pl.BlockSpec((None, P, D), lambda b, i, n: (b, i, 0)),
            pl.BlockSpec((None, 1, D), lambda b, i, n: (b, 0, 0)),
            pl.BlockSpec((None, 1, D), lambda b, i, n: (b, 0, 0)),
            pl.BlockSpec((D, ncols * D), lambda b, i, n: (0, 0), pipeline_mode=pl.Buffered(1)),
        ],
        out_specs=pl.BlockSpec((INPROJ_COLS_PER_STEP, None, None, ncw, P, COL_W),
                               lambda b, i, n: (n, b, i, 0, 0, 0)),
        out_shape=jax.ShapeDtypeStruct((ncols, B, nblk, ncw, P, COL_W), BF16),
        scratch_shapes=[pltpu.VMEM((N_GROUPS, P, D), BF16),
                        pltpu.VMEM((D // LANES, P, LANES), F32)],
        compiler_params=pltpu.CompilerParams(
            dimension_semantics=("arbitrary", "arbitrary", "arbitrary"),
            vmem_limit_bytes=VMEM_LIMIT),
        name="attn_inproj",
    )(x, sc, sh, w_bf16)


def _attn_kernel(slopes_ref, *refs):
    P = TOKEN_BLOCK
    group_refs = [refs[7 * g:7 * g + 7] for g in range(N_GROUPS)]
    gate_ref = refs[7 * N_GROUPS]
    y_ref = refs[7 * N_GROUPS + 1]
    scr = refs[7 * N_GROUPS + 2:]
    bias_scr, s_scr, p_scr, o_scr, l_scr = scr

    i = pl.program_id(1)
    hq = pl.program_id(2)
    nblk = pl.num_programs(1)

    lane = lax.broadcasted_iota(jnp.int32, (1, COL_W), 1)
    head_masks = [(lane >= h * HEAD_DIM) & (lane < (h + 1) * HEAD_DIM)
                  for h in range(HEADS_PER_STEP)]
    kk = lax.broadcasted_iota(jnp.int32, (K_TILE, COL_W), 0)
    qq = lax.broadcasted_iota(jnp.int32, (K_TILE, COL_W), 1) % Q_CHUNK
    rel = jnp.abs(kk - HALF_WIN - qq)
    dist = rel.astype(F32)
    band = jnp.where(rel <= HALF_WIN, 0.0, NEG_INF).astype(F32)
    lo_pen = jnp.where(kk < HALF_WIN, NEG_INF, 0.0).astype(F32)
    hi_pen = jnp.where(kk >= K_TILE - HALF_WIN, NEG_INF, 0.0).astype(F32)
    pair_lo = lax.broadcasted_iota(jnp.int32, (1, LANES), 1) < HEAD_DIM
    nt_dims = (((1,), (1,)), ((), ()))
    tn_dims = (((0,), (0,)), ((), ()))
    n_units = P // Q_CHUNK
    n_steps = n_units // UNITS_PER_STEP
    assert n_steps % 2 == 0
    lse_rows = lax.broadcasted_iota(jnp.int32, (P_ROWS - K_TILE, LANES), 0) < 3
    rhs_tail = jnp.concatenate(
        [jnp.zeros((P_ROWS - K_TILE, LANES), BF16), jnp.where(lse_rows, 1.0, 0.0).astype(BF16)],
        axis=1)
    p_scr[:, :, K_TILE:, :] = jnp.zeros((2, UNITS_PER_STEP, P_ROWS - K_TILE, COL_W), BF16)

    for g, d in enumerate(DILATIONS):
        q_ref, k_ref, v_ref, kp_ref, kn_ref, vp_ref, vn_ref = group_refs[g]
        rows = P // d
        nc = rows // Q_CHUNK

        def halo_tile(u, cur, prev, nxt, nc=nc):
            r, c = u // nc, u % nc
            if 0 < c < nc - 1:
                return cur[r, (c - 1) * Q_CHUNK:(c + 2) * Q_CHUNK, :]
            lo = prev[r] if c == 0 else cur[r, (c - 1) * Q_CHUNK:c * Q_CHUNK, :]
            mid = cur[r, c * Q_CHUNK:(c + 1) * Q_CHUNK, :]
            hi = nxt[r] if c == nc - 1 else cur[r, (c + 1) * Q_CHUNK:(c + 2) * Q_CHUNK, :]
            return jnp.concatenate([lo, mid, hi], axis=0)

        slope_lane = jnp.zeros((1, COL_W), F32)
        for h in range(HEADS_PER_STEP):
            slope = slopes_ref[g * HEADS + hq * HEADS_PER_STEP + h] * float(d)
            slope_lane = jnp.where(head_masks[h], slope, slope_lane)
        base = (band - slope_lane * dist) * LOG2E
        bias_scr[0] = base
        bias_scr[1] = base + lo_pen
        bias_scr[2] = base + hi_pen

        def coords(u, d=d, nc=nc):
            r, c = u // nc, u % nc
            off = c * Q_CHUNK
            tok0 = c * (Q_CHUNK * d) + r
            if not isinstance(u, int):
                off = pl.multiple_of(off, Q_CHUNK)
            if d == 1:
                dst = pl.ds(off, Q_CHUNK)
            else:
                dst = pl.ds(tok0, Q_CHUNK, stride=d)
            return r, c, off, dst

        def qk(u, s_buf, q_ref=q_ref, k_ref=k_ref, kp_ref=kp_ref, kn_ref=kn_ref,
               halo_tile=halo_tile, coords=coords, nc=nc):
            r, _, off, _ = coords(u)
            q = q_ref[r, pl.ds(off, Q_CHUNK), :]
            kt = halo_tile(u, k_ref, kp_ref, kn_ref)
            zero = jnp.zeros_like(q)
            qs = jnp.concatenate([jnp.where(hm, q, zero) for hm in head_masks], axis=0)
            _, c, _, _ = coords(u)
            var = jnp.where((i == 0) & (c == 0), 1,
                            jnp.where((i == nblk - 1) & (c == nc - 1), 2, 0))
            s_buf[...] = (lax.dot_general(kt, qs, nt_dims, preferred_element_type=F32)
                          + bias_scr[var])

        def softmax(u, s_buf, p_buf):
            s = s_buf[...]
            m = jnp.max(s, axis=0, keepdims=True)
            p = jnp.exp2(s - m)
            den = jnp.sum(p, axis=0, keepdims=True)
            p_buf[0:K_TILE, :] = (p * (1.0 / den)).astype(BF16)
            lse = m + jnp.log(den) * LOG2E
            hi = lse.astype(BF16)
            rest = lse - hi.astype(F32)
            mid = rest.astype(BF16)
            lo = (rest - mid.astype(F32)).astype(BF16)
            p_buf[K_TILE:K_TILE + BF16_ROWS, :] = jnp.concatenate(
                [hi, mid, lo, jnp.zeros((BF16_ROWS - 3, COL_W), BF16)], axis=0)

        def pv(u, p_buf, g=g, v_ref=v_ref, vp_ref=vp_ref, vn_ref=vn_ref,
               halo_tile=halo_tile, coords=coords):
            r, _, off, dst = coords(u)
            vt = halo_tile(u, v_ref, vp_ref, vn_ref)
            for k in range(HEADS_PER_STEP // 2):
                cs = slice(k * LANES, (k + 1) * LANES)
                rhs = jnp.concatenate(
                    [jnp.concatenate([vt[:, cs], jnp.zeros((K_TILE, LANES), BF16)], axis=1),
                     rhs_tail], axis=0)
                out = lax.dot_general(p_buf[:, cs], rhs, tn_dims, preferred_element_type=F32)
                even, odd = out[0:Q_CHUNK], out[Q_CHUNK:2 * Q_CHUNK]
                o_scr[g, k, dst, :] = jnp.where(pair_lo, even[:, 0:LANES], odd[:, 0:LANES])
                l_scr[g, k, dst, :] = jnp.where(pair_lo, even[:, LANES:], odd[:, LANES:])

        def step(t, par, do_pv=True, do_sm=True, do_qk=True, qk=qk, softmax=softmax, pv=pv):
            for j in range(UNITS_PER_STEP):
                if do_pv:
                    pv(UNITS_PER_STEP * (t - 2) + j, p_scr.at[par, j])
                if do_qk:
                    qk(UNITS_PER_STEP * t + j, s_scr.at[par, j])
            for j in range(UNITS_PER_STEP):
                if do_sm:
                    softmax(UNITS_PER_STEP * (t - 1) + j, s_scr.at[1 - par, j], p_scr.at[1 - par, j])

        step(0, 0, do_pv=False, do_sm=False)
        step(1, 1, do_pv=False)

        for t in range(2, n_steps):
            step(t, t % 2)
        step(n_steps, 0, do_qk=False)
        step(n_steps + 1, 1, do_sm=False, do_qk=False)

    mrows = 128

    def merge(t, carry):
        sl = pl.ds(pl.multiple_of(t * mrows, mrows), mrows)
        for cc in range(COL_W // LANES):
            cs = slice(cc * LANES, (cc + 1) * LANES)
            l0, l1, l2 = l_scr[0, cc, sl, :], l_scr[1, cc, sl, :], l_scr[2, cc, sl, :]
            mx = jnp.maximum(jnp.maximum(l0, l1), l2)
            e0, e1, e2 = jnp.exp2(l0 - mx), jnp.exp2(l1 - mx), jnp.exp2(l2 - mx)
            inv = 1.0 / (e0 + e1 + e2)
            o = (e0 * o_scr[0, cc, sl, :] + e1 * o_scr[1, cc, sl, :]
                 + e2 * o_scr[2, cc, sl, :]) * inv
            gt = gate_ref[sl, cs].astype(F32)
            y_ref[sl, cs] = (o * _silu(gt)).astype(BF16)
        return carry

    lax.fori_loop(0, P // mrows, merge, 0)


def _attention(z, slopes, B, S, D):
    P = TOKEN_BLOCK
    nblk = S // P
    ncw = D // COL_W
    ncols = z.shape[0]
    in_specs = []
    operands = []
    scratch = []
    for g, d in enumerate(DILATIONS):
        rows = P // d
        nh = rows // HALF_WIN
        zg = z.reshape(ncols, B, nblk, ncw, d, rows, COL_W)
        zh = z.reshape(ncols, B, nblk, ncw, d, nh, HALF_WIN, COL_W)
        full = (None, None, None, None, d, rows, COL_W)
        halo = (None, None, None, None, d, None, HALF_WIN, COL_W)

        def cur(n):
            return pl.BlockSpec(full, lambda b, i, hq, s, n=n: (n, b, i, hq, 0, 0, 0))

        def prev(n, nh=nh):
            return pl.BlockSpec(halo, lambda b, i, hq, s, n=n, nh=nh:
                                (n, b, jnp.maximum(i - 1, 0), hq, 0, nh - 1, 0, 0))

        def nxt(n, nblk=nblk):
            return pl.BlockSpec(halo, lambda b, i, hq, s, n=n, nblk=nblk:
                                (n, b, jnp.minimum(i + 1, nblk - 1), hq, 0, 0, 0, 0))

        nq, nk, nv = 3 * g, 3 * g + 1, 3 * g + 2
        in_specs += [cur(nq), cur(nk), cur(nv), prev(nk), nxt(nk), prev(nv), nxt(nv)]
        operands += [zg, zg, zg, zh, zh, zh, zh]
    zgate = z.reshape(ncols, B, nblk, ncw, P, COL_W)
    in_specs.append(pl.BlockSpec((None, None, None, None, P, COL_W),
                                 lambda b, i, hq, s: (ncols - 1, b, i, hq, 0, 0)))
    operands.append(zgate)
    scratch += [
        pltpu.VMEM((3, K_TILE, COL_W), F32),
        pltpu.VMEM((2, UNITS_PER_STEP, K_TILE, COL_W), F32),
        pltpu.VMEM((2, UNITS_PER_STEP, P_ROWS, COL_W), BF16),
        pltpu.VMEM((N_GROUPS, COL_W // LANES, P, LANES), F32),
        pltpu.VMEM((N_GROUPS, COL_W // LANES, P, LANES), F32),
    ]
    grid_spec = pltpu.PrefetchScalarGridSpec(
        num_scalar_prefetch=1,
        grid=(B, nblk, ncw),
        in_specs=in_specs,
        out_specs=pl.BlockSpec((None, P, COL_W), lambda b, i, hq, s: (b, i, hq)),
        scratch_shapes=scratch,
    )
    return pl.pallas_call(
        _attn_kernel,
        grid_spec=grid_spec,
        out_shape=jax.ShapeDtypeStruct((B, S, D), BF16),
        compiler_params=pltpu.CompilerParams(
            dimension_semantics=("arbitrary", "arbitrary", "arbitrary"),
            vmem_limit_bytes=VMEM_LIMIT),
        name="dilated_attn",
    )(slopes, *operands)


SGU_COLS = 256


def _gelu_exact(x):
    return 0.5 * x * (1.0 + lax.erf(x * (2.0 ** -0.5)))


def _sgu_kernel(ya_ref, x_ref, gca_ref, wa_ref, lga_ref, lba_ref,
                sc_ref, sh_ref, gc_ref, win_ref, lng_ref, lnb_ref, ws_ref, bs_ref,
                wout_ref, plg_ref, plb_ref, o_ref, x1_scr, h_scr, v_scr, y_scr):
    tm = x_ref.shape[0]
    E = v_scr.shape[1]

    out0 = jnp.dot(ya_ref[...], wa_ref[...], preferred_element_type=F32)
    res0 = DEEPNORM_ALPHA * x_ref[...] + gca_ref[...] * out0
    x1_scr[...] = _layer_norm_rows(res0, lga_ref[...], lba_ref[...])

    h_scr[...] = (x1_scr[...] * sc_ref[...] + sh_ref[...]).astype(BF16)

    zv = jnp.dot(h_scr[...], win_ref[:, E:2 * E], preferred_element_type=F32)
    v = _layer_norm_rows(_gelu_exact(zv), lng_ref[...], lnb_ref[...])
    v_scr[...] = v.astype(BF16)

    gpb = SGU_COLS // SGU_GROUP_CH
    for cb in range(E // SGU_COLS):
        c0 = cb * SGU_COLS
        zu = jnp.dot(h_scr[...], win_ref[:, c0:c0 + SGU_COLS], preferred_element_type=F32)
        zg = jnp.dot(h_scr[...], win_ref[:, 2 * E + c0:2 * E + c0 + SGU_COLS],
                     preferred_element_type=F32)
        ug = _gelu_exact(zu) * _silu(zg)
        for gi in range(gpb):
            g = cb * gpb + gi
            l0 = gi * SGU_GROUP_CH
            wsg = ws_ref[g]
            bsg = bs_ref[:, g:g + 1]
            gcols = slice(c0 + l0, c0 + l0 + SGU_GROUP_CH)
            for nn in range(0, tm // SGU_CHUNK, 2):
                ra = slice(nn * SGU_CHUNK, (nn + 1) * SGU_CHUNK)
                rb = slice((nn + 1) * SGU_CHUNK, (nn + 2) * SGU_CHUNK)
                vc = jnp.concatenate([v_scr[ra, gcols], v_scr[rb, gcols]], axis=1)
                sv = jnp.dot(wsg, vc, preferred_element_type=F32) + bsg
                for rr, half in ((ra, 0), (rb, 1)):
                    y = (ug[rr, l0:l0 + SGU_GROUP_CH]
                         * sv[:, half * SGU_GROUP_CH:(half + 1) * SGU_GROUP_CH])
                    y_scr[rr, gcols] = y.astype(BF16)

    out = jnp.dot(y_scr[...], wout_ref[...], preferred_element_type=F32)
    res = DEEPNORM_ALPHA * x1_scr[...] + gc_ref[...] * out
    o_ref[...] = _layer_norm_rows(res, plg_ref[...], plb_ref[...])


def _outproj_sgu_layer(y_a, x, gate_c_a, w_out_a, post_g_a, post_b_a,
                       sc, sh, gate_c, w_in, ln_g, ln_b, w_s, b_s_t, w_out, post_g, post_b):
    B, S, D = x.shape
    E = w_out.shape[0]
    G = w_s.shape[0]
    tm = 512
    const2 = lambda b, i: (0, 0)
    single = pl.Buffered(1)
    return pl.pallas_call(
        _sgu_kernel,
        grid=(B, S // tm),
        in_specs=[
            pl.BlockSpec((None, tm, D), lambda b, i: (b, i, 0)),
            pl.BlockSpec((None, tm, D), lambda b, i: (b, i, 0)),
            pl.BlockSpec((None, 1, D), lambda b, i: (b, 0, 0)),
            pl.BlockSpec((D, D), const2, pipeline_mode=single),
            pl.BlockSpec((1, D), const2),
            pl.BlockSpec((1, D), const2),
            pl.BlockSpec((None, 1, D), lambda b, i: (b, 0, 0)),
            pl.BlockSpec((None, 1, D), lambda b, i: (b, 0, 0)),
            pl.BlockSpec((None, 1, D), lambda b, i: (b, 0, 0)),
            pl.BlockSpec((D, 3 * E), const2, pipeline_mode=single),
            pl.BlockSpec((1, E), const2),
            pl.BlockSpec((1, E), const2),
            pl.BlockSpec((G, SGU_CHUNK, SGU_CHUNK), lambda b, i: (0, 0, 0), pipeline_mode=single),
            pl.BlockSpec((SGU_CHUNK, G), const2),
            pl.BlockSpec((E, D), const2, pipeline_mode=single),
            pl.BlockSpec((1, D), const2),
            pl.BlockSpec((1, D), const2),
        ],
        out_specs=pl.BlockSpec((None, tm, D), lambda b, i: (b, i, 0)),
        out_shape=jax.ShapeDtypeStruct((B, S, D), F32),
        scratch_shapes=[
            pltpu.VMEM((tm, D), F32),
            pltpu.VMEM((tm, D), BF16),
            pltpu.VMEM((tm, E), BF16),
            pltpu.VMEM((tm, E), BF16),
        ],
        compiler_params=pltpu.CompilerParams(
            dimension_semantics=("arbitrary", "arbitrary"),
            vmem_limit_bytes=VMEM_LIMIT),
        name="outproj_sgu_layer",
    )(y_a, x, gate_c_a, w_out_a, post_g_a, post_b_a,
      sc, sh, gate_c, w_in, ln_g, ln_b, w_s, b_s_t, w_out, post_g, post_b)


def _alibi_slopes(n):
    return 2.0 ** (-8.0 * jnp.arange(1, n + 1, dtype=F32) / n)


def kernel(x, c, ada_w, ada_b, post_ln_g, post_ln_b, a_w_in, a_w_out, b_w_in, b_ln_g, b_ln_b,
           b_w_s, b_b_s, b_w_out):
    B, S, D = x.shape
    mod = _modulation(c, ada_w, ada_b)
    slopes = _alibi_slopes(N_GROUPS * HEADS)

    def mod_parts(i):
        shift = mod[i, :, None, 0:D]
        scale1 = 1.0 + mod[i, :, None, D:2 * D]
        gate_c = mod[i, :, None, 2 * D:3 * D]
        return scale1, shift, gate_c

    sc, sh, gc_a = mod_parts(0)
    z = _attn_inproj(x, sc, sh, a_w_in[0].astype(BF16))
    y = _attention(z, slopes, B, S, D)

    sc, sh, gc = mod_parts(1)
    return _outproj_sgu_layer(
        y, x, gc_a, a_w_out[0].astype(BF16), post_ln_g[0][None, :], post_ln_b[0][None, :],
        sc, sh, gc, b_w_in[0].astype(BF16), b_ln_g[0][None, :], b_ln_b[0][None, :],
        b_w_s[0].astype(BF16), b_b_s[0].T, b_w_out[0].astype(BF16),
        post_ln_g[1][None, :], post_ln_b[1][None, :])
```

```python
import functools

import numpy as np
import jax
import jax.numpy as jnp
from jax import lax
from jax.experimental import pallas as pl
from jax.experimental.pallas import tpu as pltpu

F32 = jnp.float32
BF16 = jnp.bfloat16

DEPTH = 2
ATTN_GROUPS = ((128, 1), (512, 4), (2048, 16))
N_GROUPS = len(ATTN_GROUPS)
DILATIONS = tuple(d for _, d in ATTN_GROUPS)
HEADS = 16
HEAD_DIM = 64
HALF_WIN = 64
SGU_CHUNK = 128
SGU_GROUP_CH = 128
DEEPNORM_ALPHA = (2 * DEPTH) ** 0.25
LN_EPS = 1e-5
NEG_INF = -1e30
LOG2E = 1.4426950408889634
LANES = 128

TOKEN_BLOCK = 1024
HEADS_PER_STEP = 4
COL_W = HEADS_PER_STEP * HEAD_DIM
Q_CHUNK = 64
K_TILE = Q_CHUNK + 2 * HALF_WIN
UNITS_PER_STEP = 4
BF16_ROWS = 16
SUBLANES = 8
SM_ROWS = 32
P_ROWS = 256
VMEM_LIMIT = 56 * 1024 * 1024

assert all(w // (2 * d) == HALF_WIN for w, d in ATTN_GROUPS)


def _silu(x):
    return x * jax.nn.sigmoid(x)


def _layer_norm_rows(x, g, b):
    mu = jnp.mean(x, axis=-1, keepdims=True)
    xc = x - mu
    var = jnp.mean(xc * xc, axis=-1, keepdims=True)
    return xc * lax.rsqrt(var + LN_EPS) * g + b


def _mod_kernel(ct_ref, w_ref, b_ref, o_ref):
    ct = ct_ref[...]
    cond = _silu(ct)
    w = w_ref[...]
    for b in range(ct.shape[1]):
        col = cond[:, b:b + 1]
        o_ref[b:b + 1, :] = jnp.sum(col * w, axis=0, keepdims=True) + b_ref[...]


def _modulation(c, ada_w, ada_b):
    B, D = c.shape
    depth, _, n3 = ada_w.shape
    tn = 512
    return pl.pallas_call(
        _mod_kernel,
        grid=(depth, n3 // tn),
        in_specs=[
            pl.BlockSpec((D, B), lambda i, n: (0, 0)),
            pl.BlockSpec((None, D, tn), lambda i, n: (i, 0, n)),
            pl.BlockSpec((None, 1, tn), lambda i, n: (i, 0, n)),
        ],
        out_specs=pl.BlockSpec((None, B, tn), lambda i, n: (i, 0, n)),
        out_shape=jax.ShapeDtypeStruct((depth, B, n3), F32),
        compiler_params=pltpu.CompilerParams(
            dimension_semantics=("arbitrary", "arbitrary")),
        name="adaln_mod",
    )(c.T, ada_w, ada_b.reshape(depth, 1, n3))


def _inproj_kernel(x_ref, sc_ref, sh_ref, w_ref, z_ref, h_scr, xs_scr):
    step = pl.program_id(2)
    tm, D = x_ref.shape

    def group_step(g, d):
        rows = tm // d
        for cc in range(D // LANES):
            cs = slice(cc * LANES, (cc + 1) * LANES)
            if g == 0:
                h = x_ref[:, cs] * sc_ref[:, cs] + sh_ref[:, cs]
                xs_scr[cc] = h
                h_scr[:, cs] = h.astype(BF16)
            else:
                for r in range(d):
                    hr = xs_scr[cc, pl.ds(r, rows, stride=d), :]
                    h_scr[r * rows:(r + 1) * rows, cs] = hr.astype(BF16)
        for j in range(3):
            n = 3 * g + j
            z = jnp.dot(h_scr[...], w_ref[:, n * D:(n + 1) * D], preferred_element_type=F32)
            if j == 0:
                z = z * (HEAD_DIM ** -0.5 * LOG2E)
            z = z.astype(BF16)
            for cc in range(z_ref.shape[1]):
                z_ref[j, cc] = z[:, cc * COL_W:(cc + 1) * COL_W]

    for g, d in enumerate(DILATIONS):
        pl.when(step == g)(functools.partial(group_step, g, d))


def _attn_inproj(x, sc, sh, w_bf16):
    B, S, D = x.shape
    P = TOKEN_BLOCK
    nblk = S // P
    ncols = 3 * N_GROUPS
    ncw = D // COL_W
    return pl.pallas_call(
        _inproj_kernel,
        grid=(B, nblk, N_GROUPS),
        in_specs=[
            pl.BlockSpec((None, P, D), lambda b, i, g: (b, i, 0)),
            pl.BlockSpec((None, 1, D), lambda b, i, g: (b, 0, 0)),
            pl.BlockSpec((None, 1, D), lambda b, i, g: (b, 0, 0)),
            pl.BlockSpec((D, ncols * D), lambda b, i, g: (0, 0), pipeline_mode=pl.Buffered(1)),
        ],
        out_specs=pl.BlockSpec((3, None, None, ncw, P, COL_W),
                               lambda b, i, g: (g, b, i, 0, 0, 0)),
        out_shape=jax.ShapeDtypeStruct((ncols, B, nblk, ncw, P, COL_W), BF16),
        scratch_shapes=[pltpu.VMEM((P, D), BF16),
                        pltpu.VMEM((D // LANES, P, LANES), F32)],
        compiler_params=pltpu.CompilerParams(
            dimension_semantics=("arbitrary", "arbitrary", "arbitrary"),
            vmem_limit_bytes=VMEM_LIMIT),
        name="attn_inproj",
    )(x, sc, sh, w_bf16)


def _gate_kernel(x_ref, sc_ref, sh_ref, w_ref, o_ref):
    h = (x_ref[...] * sc_ref[...] + sh_ref[...]).astype(BF16)
    sg = _silu(jnp.dot(h, w_ref[...], preferred_element_type=F32)).astype(BF16)
    for cc in range(o_ref.shape[0]):
        o_ref[cc] = sg[:, cc * COL_W:(cc + 1) * COL_W]


def _attn_gate(x, sc, sh, w_bf16):
    B, S, D = x.shape
    P = TOKEN_BLOCK
    nblk = S // P
    ncw = D // COL_W
    gate_col = w_bf16.shape[1] // D - 1
    return pl.pallas_call(
        _gate_kernel,
        grid=(B, nblk),
        in_specs=[
            pl.BlockSpec((None, P, D), lambda b, i: (b, i, 0)),
            pl.BlockSpec((None, 1, D), lambda b, i: (b, 0, 0)),
            pl.BlockSpec((None, 1, D), lambda b, i: (b, 0, 0)),
            pl.BlockSpec((D, D), lambda b, i: (0, gate_col)),
        ],
        out_specs=pl.BlockSpec((None, None, ncw, P, COL_W), lambda b, i: (b, i, 0, 0, 0)),
        out_shape=jax.ShapeDtypeStruct((B, nblk, ncw, P, COL_W), BF16),
        compiler_params=pltpu.CompilerParams(
            dimension_semantics=("arbitrary", "arbitrary"),
            vmem_limit_bytes=VMEM_LIMIT),
        name="attn_gate",
    )(x, sc, sh, w_bf16)


def _attn_kernel(slopes_ref, *refs):
    P = TOKEN_BLOCK
    group_refs = [refs[7 * g:7 * g + 7] for g in range(N_GROUPS)]
    gate_ref = refs[7 * N_GROUPS]
    y_ref = refs[7 * N_GROUPS + 1]
    scr = refs[7 * N_GROUPS + 2:]
    bias_scr, s_scr, p_scr, o_scr, l_scr = scr

    i = pl.program_id(1)
    hq = pl.program_id(2)
    nblk = pl.num_programs(1)

    lane = lax.broadcasted_iota(jnp.int32, (1, COL_W), 1)
    head_masks = [(lane >= h * HEAD_DIM) & (lane < (h + 1) * HEAD_DIM)
                  for h in range(HEADS_PER_STEP)]
    kk = lax.broadcasted_iota(jnp.int32, (K_TILE, COL_W), 0)
    qq = lax.broadcasted_iota(jnp.int32, (K_TILE, COL_W), 1) % Q_CHUNK
    rel = jnp.abs(kk - HALF_WIN - qq)
    dist = rel.astype(F32)
    band = jnp.where(rel <= HALF_WIN, 0.0, NEG_INF).astype(F32)
    lo_pen = jnp.where(kk < HALF_WIN, NEG_INF, 0.0).astype(F32)
    hi_pen = jnp.where(kk >= K_TILE - HALF_WIN, NEG_INF, 0.0).astype(F32)
    pair_lo = lax.broadcasted_iota(jnp.int32, (1, LANES), 1) < HEAD_DIM
    nt_dims = (((1,), (1,)), ((), ()))
    tn_dims = (((0,), (0,)), ((), ()))
    n_units = P // Q_CHUNK
    n_steps = n_units // UNITS_PER_STEP
    assert n_steps % 2 == 0
    lse_rows = lax.broadcasted_iota(jnp.int32, (P_ROWS - K_TILE, LANES), 0) < 3
    rhs_tail = jnp.concatenate(
        [jnp.zeros((P_ROWS - K_TILE, LANES), BF16), jnp.where(lse_rows, 1.0, 0.0).astype(BF16)],
        axis=1)
    p_scr[:, :, K_TILE:, :] = jnp.zeros((2, UNITS_PER_STEP, P_ROWS - K_TILE, COL_W), BF16)

    for g, d in enumerate(DILATIONS):
        q_ref, k_ref, v_ref, kp_ref, kn_ref, vp_ref, vn_ref = group_refs[g]
        rows = P // d
        nc = rows // Q_CHUNK

        def halo_tile(u, cur, prev, nxt, nc=nc):
            r, c = u // nc, u % nc
            if 0 < c < nc - 1:
                return cur[r, (c - 1) * Q_CHUNK:(c + 2) * Q_CHUNK, :]
            lo = prev[r] if c == 0 else cur[r, (c - 1) * Q_CHUNK:c * Q_CHUNK, :]
            mid = cur[r, c * Q_CHUNK:(c + 1) * Q_CHUNK, :]
            hi = nxt[r] if c == nc - 1 else cur[r, (c + 1) * Q_CHUNK:(c + 2) * Q_CHUNK, :]
            return jnp.concatenate([lo, mid, hi], axis=0)

        slope_lane = jnp.zeros((1, COL_W), F32)
        for h in range(HEADS_PER_STEP):
            slope = slopes_ref[g * HEADS + hq * HEADS_PER_STEP + h] * float(d)
            slope_lane = jnp.where(head_masks[h], slope, slope_lane)
        base = (band - slope_lane * dist) * LOG2E
        bias_scr[0] = base
        bias_scr[1] = base + lo_pen
        bias_scr[2] = base + hi_pen

        def coords(u, d=d, nc=nc):
            r, c = u // nc, u % nc
            off = c * Q_CHUNK
            tok0 = c * (Q_CHUNK * d) + r
            if not isinstance(u, int):
                off = pl.multiple_of(off, Q_CHUNK)
            if d == 1:
                dst = pl.ds(off, Q_CHUNK)
            else:
                dst = pl.ds(tok0, Q_CHUNK, stride=d)
            return r, c, off, dst

        def qk(u, s_buf, q_ref=q_ref, k_ref=k_ref, kp_ref=kp_ref, kn_ref=kn_ref,
               halo_tile=halo_tile, coords=coords, nc=nc):
            r, _, off, _ = coords(u)
            q = q_ref[r, pl.ds(off, Q_CHUNK), :]
            kt = halo_tile(u, k_ref, kp_ref, kn_ref)
            zero = jnp.zeros_like(q)
            qs = jnp.concatenate([jnp.where(hm, q, zero) for hm in head_masks], axis=0)
            _, c, _, _ = coords(u)
            var = jnp.where((i == 0) & (c == 0), 1,
                            jnp.where((i == nblk - 1) & (c == nc - 1), 2, 0))
            s_buf[...] = (lax.dot_general(kt, qs, nt_dims, preferred_element_type=F32)
                          + bias_scr[var])

        def softmax(u, s_buf, p_buf):
            s = s_buf[...]
            m = jnp.max(s, axis=0, keepdims=True)
            p = jnp.exp2(s - m)
            den = jnp.sum(p, axis=0, keepdims=True)
            p_buf[0:K_TILE, :] = (p * (1.0 / den)).astype(BF16)
            lse = m + jnp.log(den) * LOG2E
            hi = lse.astype(BF16)
            rest = lse - hi.astype(F32)
            mid = rest.astype(BF16)
            lo = (rest - mid.astype(F32)).astype(BF16)
            p_buf[K_TILE:K_TILE + BF16_ROWS, :] = jnp.concatenate(
                [hi, mid, lo, jnp.zeros((BF16_ROWS - 3, COL_W), BF16)], axis=0)

        def pv(u, p_buf, g=g, v_ref=v_ref, vp_ref=vp_ref, vn_ref=vn_ref,
               halo_tile=halo_tile, coords=coords):
            r, _, off, dst = coords(u)
            vt = halo_tile(u, v_ref, vp_ref, vn_ref)
            for k in range(HEADS_PER_STEP // 2):
                cs = slice(k * LANES, (k + 1) * LANES)
                rhs = jnp.concatenate(
                    [jnp.concatenate([vt[:, cs], jnp.zeros((K_TILE, LANES), BF16)], axis=1),
                     rhs_tail], axis=0)
                out = lax.dot_general(p_buf[:, cs], rhs, tn_dims, preferred_element_type=F32)
                even, odd = out[0:Q_CHUNK], out[Q_CHUNK:2 * Q_CHUNK]
                o_scr[g, k, dst, :] = jnp.where(pair_lo, even[:, 0:LANES], odd[:, 0:LANES])
                l_scr[g, k, dst, :] = jnp.where(pair_lo, even[:, LANES:], odd[:, LANES:])

        def step(t, par, do_pv=True, do_sm=True, do_qk=True, qk=qk, softmax=softmax, pv=pv):
            for j in range(UNITS_PER_STEP):
                if do_pv:
                    pv(UNITS_PER_STEP * (t - 2) + j, p_scr.at[par, j])
                if do_qk:
                    qk(UNITS_PER_STEP * t + j, s_scr.at[par, j])
            for j in range(UNITS_PER_STEP):
                if do_sm:
                    softmax(UNITS_PER_STEP * (t - 1) + j, s_scr.at[1 - par, j], p_scr.at[1 - par, j])

        step(0, 0, do_pv=False, do_sm=False)
        step(1, 1, do_pv=False)

        for t in range(2, n_steps):
            step(t, t % 2)
        step(n_steps, 0, do_qk=False)
        step(n_steps + 1, 1, do_sm=False, do_qk=False)

    mrows = 128

    def merge(t, carry):
        sl = pl.ds(pl.multiple_of(t * mrows, mrows), mrows)
        for cc in range(COL_W // LANES):
            cs = slice(cc * LANES, (cc + 1) * LANES)
            l0, l1, l2 = l_scr[0, cc, sl, :], l_scr[1, cc, sl, :], l_scr[2, cc, sl, :]
            mx = jnp.maximum(jnp.maximum(l0, l1), l2)
            e0, e1, e2 = jnp.exp2(l0 - mx), jnp.exp2(l1 - mx), jnp.exp2(l2 - mx)
            inv = 1.0 / (e0 + e1 + e2)
            o = (e0 * o_scr[0, cc, sl, :] + e1 * o_scr[1, cc, sl, :]
                 + e2 * o_scr[2, cc, sl, :]) * inv
            y_ref[sl, cs] = (o * gate_ref[sl, cs].astype(F32)).astype(BF16)
        return carry

    lax.fori_loop(0, P // mrows, merge, 0)


def _attention(z, silu_gate, slopes, B, S, D):
    P = TOKEN_BLOCK
    nblk = S // P
    ncw = D // COL_W
    ncols = z.shape[0]
    in_specs = []
    operands = []
    scratch = []
    for g, d in enumerate(DILATIONS):
        rows = P // d
        nh = rows // HALF_WIN
        zg = z.reshape(ncols, B, nblk, ncw, d, rows, COL_W)
        zh = z.reshape(ncols, B, nblk, ncw, d, nh, HALF_WIN, COL_W)
        full = (None, None, None, None, d, rows, COL_W)
        halo = (None, None, None, None, d, None, HALF_WIN, COL_W)

        def cur(n):
            return pl.BlockSpec(full, lambda b, i, hq, s, n=n: (n, b, i, hq, 0, 0, 0))

        def prev(n, nh=nh):
            return pl.BlockSpec(halo, lambda b, i, hq, s, n=n, nh=nh:
                                (n, b, jnp.maximum(i - 1, 0), hq, 0, nh - 1, 0, 0))

        def nxt(n, nblk=nblk):
            return pl.BlockSpec(halo, lambda b, i, hq, s, n=n, nblk=nblk:
                                (n, b, jnp.minimum(i + 1, nblk - 1), hq, 0, 0, 0, 0))

        nq, nk, nv = 3 * g, 3 * g + 1, 3 * g + 2
        in_specs += [cur(nq), cur(nk), cur(nv), prev(nk), nxt(nk), prev(nv), nxt(nv)]
        operands += [zg, zg, zg, zh, zh, zh, zh]
    in_specs.append(pl.BlockSpec((None, None, None, P, COL_W),
                                 lambda b, i, hq, s: (b, i, hq, 0, 0)))
    operands.append(silu_gate)
    scratch += [
        pltpu.VMEM((3, K_TILE, COL_W), F32),
        pltpu.VMEM((2, UNITS_PER_STEP, K_TILE, COL_W), F32),
        pltpu.VMEM((2, UNITS_PER_STEP, P_ROWS, COL_W), BF16),
        pltpu.VMEM((N_GROUPS, COL_W // LANES, P, LANES), F32),
        pltpu.VMEM((N_GROUPS, COL_W // LANES, P, LANES), F32),
    ]
    grid_spec = pltpu.PrefetchScalarGridSpec(
        num_scalar_prefetch=1,
        grid=(B, nblk, ncw),
        in_specs=in_specs,
        out_specs=pl.BlockSpec((None, P, COL_W), lambda b, i, hq, s: (b, i, hq)),
        scratch_shapes=scratch,
    )
    return pl.pallas_call(
        _attn_kernel,
        grid_spec=grid_spec,
        out_shape=jax.ShapeDtypeStruct((B, S, D), BF16),
        compiler_params=pltpu.CompilerParams(
            dimension_semantics=("arbitrary", "arbitrary", "arbitrary"),
            vmem_limit_bytes=VMEM_LIMIT),
        name="dilated_attn",
    )(slopes, *operands)


SGU_COLS = 256


def _gelu_exact(x):
    return 0.5 * x * (1.0 + lax.erf(x * (2.0 ** -0.5)))


def _sgu_kernel(ya_ref, x_ref, gca_ref, wa_ref, lga_ref, lba_ref,
                sc_ref, sh_ref, gc_ref, win_ref, lng_ref, lnb_ref, ws_ref, bs_ref,
                wout_ref, plg_ref, plb_ref, o_ref, x1_scr, h_scr, v_scr, y_scr):
    tm = x_ref.shape[0]
    E = v_scr.shape[1]

    halves = [slice(hh * (tm // 2), (hh + 1) * (tm // 2)) for hh in range(2)]

    for rs in halves:
        out0 = jnp.dot(ya_ref[rs, :], wa_ref[...], preferred_element_type=F32)
        res0 = DEEPNORM_ALPHA * x_ref[rs, :] + gca_ref[...] * out0
        x1 = _layer_norm_rows(res0, lga_ref[...], lba_ref[...])
        x1_scr[rs, :] = x1
        h_scr[rs, :] = (x1 * sc_ref[...] + sh_ref[...]).astype(BF16)

    for rs in halves:
        zv = jnp.dot(h_scr[rs, :], win_ref[:, E:2 * E], preferred_element_type=F32)
        v = _layer_norm_rows(_gelu_exact(zv), lng_ref[...], lnb_ref[...])
        v_scr[rs, :] = v.astype(BF16)

    gpb = SGU_COLS // SGU_GROUP_CH
    for cb in range(E // SGU_COLS):
        c0 = cb * SGU_COLS
        zu = jnp.dot(h_scr[...], win_ref[:, c0:c0 + SGU_COLS], preferred_element_type=F32)
        zg = jnp.dot(h_scr[...], win_ref[:, 2 * E + c0:2 * E + c0 + SGU_COLS],
                     preferred_element_type=F32)
        ug = _gelu_exact(zu) * _silu(zg)
        for gi in range(gpb):
            g = cb * gpb + gi
            l0 = gi * SGU_GROUP_CH
            wsg = ws_ref[g]
            bsg = bs_ref[:, g:g + 1]
            gcols = slice(c0 + l0, c0 + l0 + SGU_GROUP_CH)
            for nn in range(0, tm // SGU_CHUNK, 2):
                ra = slice(nn * SGU_CHUNK, (nn + 1) * SGU_CHUNK)
                rb = slice((nn + 1) * SGU_CHUNK, (nn + 2) * SGU_CHUNK)
                vc = jnp.concatenate([v_scr[ra, gcols], v_scr[rb, gcols]], axis=1)
                sv = jnp.dot(wsg, vc, preferred_element_type=F32) + bsg
                for rr, half in ((ra, 0), (rb, 1)):
                    y = (ug[rr, l0:l0 + SGU_GROUP_CH]
                         * sv[:, half * SGU_GROUP_CH:(half + 1) * SGU_GROUP_CH])
                    y_scr[rr, gcols] = y.astype(BF16)

    for rs in halves:
        out = jnp.dot(y_scr[rs, :], wout_ref[...], preferred_element_type=F32)
        res = DEEPNORM_ALPHA * x1_scr[rs, :] + gc_ref[...] * out
        o_ref[rs, :] = _layer_norm_rows(res, plg_ref[...], plb_ref[...])


def _outproj_sgu_layer(y_a, x, gate_c_a, w_out_a, post_g_a, post_b_a,
                       sc, sh, gate_c, w_in, ln_g, ln_b, w_s, b_s_t, w_out, post_g, post_b):
    B, S, D = x.shape
    E = w_out.shape[0]
    G = w_s.shape[0]
    tm = 512
    const2 = lambda b, i: (0, 0)
    single = pl.Buffered(1)
    return pl.pallas_call(
        _sgu_kernel,
        grid=(B, S // tm),
        in_specs=[
            pl.BlockSpec((None, tm, D), lambda b, i: (b, i, 0)),
            pl.BlockSpec((None, tm, D), lambda b, i: (b, i, 0)),
            pl.BlockSpec((None, 1, D), lambda b, i: (b, 0, 0)),
            pl.BlockSpec((D, D), const2, pipeline_mode=single),
            pl.BlockSpec((1, D), const2),
            pl.BlockSpec((1, D), const2),
            pl.BlockSpec((None, 1, D), lambda b, i: (b, 0, 0)),
            pl.BlockSpec((None, 1, D), lambda b, i: (b, 0, 0)),
            pl.BlockSpec((None, 1, D), lambda b, i: (b, 0, 0)),
            pl.BlockSpec((D, 3 * E), const2, pipeline_mode=single),
            pl.BlockSpec((1, E), const2),
            pl.BlockSpec((1, E), const2),
            pl.BlockSpec((G, SGU_CHUNK, SGU_CHUNK), lambda b, i: (0, 0, 0), pipeline_mode=single),
            pl.BlockSpec((SGU_CHUNK, G), const2),
            pl.BlockSpec((E, D), const2, pipeline_mode=single),
            pl.BlockSpec((1, D), const2),
            pl.BlockSpec((1, D), const2),
        ],
        out_specs=pl.BlockSpec((None, tm, D), lambda b, i: (b, i, 0)),
        out_shape=jax.ShapeDtypeStruct((B, S, D), F32),
        scratch_shapes=[
            pltpu.VMEM((tm, D), F32),
            pltpu.VMEM((tm, D), BF16),
            pltpu.VMEM((tm, E), BF16),
            pltpu.VMEM((tm, E), BF16),
        ],
        compiler_params=pltpu.CompilerParams(
            dimension_semantics=("arbitrary", "arbitrary"),
            vmem_limit_bytes=VMEM_LIMIT),
        name="outproj_sgu_layer",
    )(y_a, x, gate_c_a, w_out_a, post_g_a, post_b_a,
      sc, sh, gate_c, w_in, ln_g, ln_b, w_s, b_s_t, w_out, post_g, post_b)


def _alibi_slopes(n):
    return 2.0 ** (-8.0 * jnp.arange(1, n + 1, dtype=F32) / n)


def kernel(x, c, ada_w, ada_b, post_ln_g, post_ln_b, a_w_in, a_w_out, b_w_in, b_ln_g, b_ln_b,
           b_w_s, b_b_s, b_w_out):
    B, S, D = x.shape
    mod = _modulation(c, ada_w, ada_b)
    slopes = _alibi_slopes(N_GROUPS * HEADS)

    def mod_parts(i):
        shift = mod[i, :, None, 0:D]
        scale1 = 1.0 + mod[i, :, None, D:2 * D]
        gate_c = mod[i, :, None, 2 * D:3 * D]
        return scale1, shift, gate_c

    sc, sh, gc_a = mod_parts(0)
    a_w_in_bf16 = a_w_in[0].astype(BF16)
    z = _attn_inproj(x, sc, sh, a_w_in_bf16)
    silu_gate = _attn_gate(x, sc, sh, a_w_in_bf16)
    y = _attention(z, silu_gate, slopes, B, S, D)

    sc, sh, gc = mod_parts(1)
    return _outproj_sgu_layer(
        y, x, gc_a, a_w_out[0].astype(BF16), post_ln_g[0][None, :], post_ln_b[0][None, :],
        sc, sh, gc, b_w_in[0].astype(BF16), b_ln_g[0][None, :], b_ln_b[0][None, :],
        b_w_s[0].astype(BF16), b_b_s[0].T, b_w_out[0].astype(BF16),
        post_ln_g[1][None, :], post_ln_b[1][None, :])
```

```python
import functools

import numpy as np
import jax
import jax.numpy as jnp
from jax import lax
from jax.experimental import pallas as pl
from jax.experimental.pallas import tpu as pltpu

F32 = jnp.float32
BF16 = jnp.bfloat16

DEPTH = 2
ATTN_GROUPS = ((128, 1), (512, 4), (2048, 16))
N_GROUPS = len(ATTN_GROUPS)
DILATIONS = tuple(d for _, d in ATTN_GROUPS)
HEADS = 16
HEAD_DIM = 64
HALF_WIN = 64
SGU_CHUNK = 128
SGU_GROUP_CH = 128
DEEPNORM_ALPHA = (2 * DEPTH) ** 0.25
LN_EPS = 1e-5
NEG_INF = -1e30
LOG2E = 1.4426950408889634
LANES = 128

TOKEN_BLOCK = 1024
HEADS_PER_STEP = 4
COL_W = HEADS_PER_STEP * HEAD_DIM
Q_CHUNK = 64
K_TILE = Q_CHUNK + 2 * HALF_WIN
UNITS_PER_STEP = 2
BF16_ROWS = 16
SUBLANES = 8
SM_ROWS = 32
P_ROWS = 256
VMEM_LIMIT = 56 * 1024 * 1024

assert all(w // (2 * d) == HALF_WIN for w, d in ATTN_GROUPS)


def _silu(x):
    return x * jax.nn.sigmoid(x)


def _layer_norm_rows(x, g, b):
    mu = jnp.mean(x, axis=-1, keepdims=True)
    xc = x - mu
    var = jnp.mean(xc * xc, axis=-1, keepdims=True)
    return xc * lax.rsqrt(var + LN_EPS) * g + b


def _mod_kernel(ct_ref, w_ref, b_ref, o_ref):
    ct = ct_ref[...]
    cond = _silu(ct)
    w = w_ref[...]
    for b in range(ct.shape[1]):
        col = cond[:, b:b + 1]
        o_ref[b:b + 1, :] = jnp.sum(col * w, axis=0, keepdims=True) + b_ref[...]


def _modulation(c, ada_w, ada_b):
    B, D = c.shape
    depth, _, n3 = ada_w.shape
    tn = 1024
    return pl.pallas_call(
        _mod_kernel,
        grid=(depth, n3 // tn),
        in_specs=[
            pl.BlockSpec((D, B), lambda i, n: (0, 0)),
            pl.BlockSpec((None, D, tn), lambda i, n: (i, 0, n)),
            pl.BlockSpec((None, 1, tn), lambda i, n: (i, 0, n)),
        ],
        out_specs=pl.BlockSpec((None, B, tn), lambda i, n: (i, 0, n)),
        out_shape=jax.ShapeDtypeStruct((depth, B, n3), F32),
        compiler_params=pltpu.CompilerParams(
            dimension_semantics=("arbitrary", "arbitrary")),
        name="adaln_mod",
    )(c.T, ada_w, ada_b.reshape(depth, 1, n3))


def _inproj_kernel(x_ref, sc_ref, sh_ref, w_ref, z_ref, sg_ref, h_scr, xs_scr):
    step = pl.program_id(2)
    tm, D = x_ref.shape

    def group_step(g, d):
        rows = tm // d
        for cc in range(D // LANES):
            cs = slice(cc * LANES, (cc + 1) * LANES)
            if g == 0:
                h = x_ref[:, cs] * sc_ref[:, cs] + sh_ref[:, cs]
                xs_scr[cc] = h
                h_scr[:, cs] = h.astype(BF16)
            else:
                for r in range(d):
                    hr = xs_scr[cc, pl.ds(r, rows, stride=d), :]
                    h_scr[r * rows:(r + 1) * rows, cs] = hr.astype(BF16)
        for j in range(3):
            n = 3 * g + j
            z = jnp.dot(h_scr[...], w_ref[:, n * D:(n + 1) * D], preferred_element_type=F32)
            if j == 0:
                z = z * (HEAD_DIM ** -0.5 * LOG2E)
            z = z.astype(BF16)
            for cc in range(z_ref.shape[1]):
                z_ref[j, cc] = z[:, cc * COL_W:(cc + 1) * COL_W]
        if g == 0:
            n = 3 * N_GROUPS
            sg = _silu(jnp.dot(h_scr[...], w_ref[:, n * D:(n + 1) * D],
                               preferred_element_type=F32)).astype(BF16)
            for cc in range(sg_ref.shape[0]):
                sg_ref[cc] = sg[:, cc * COL_W:(cc + 1) * COL_W]

    for g, d in enumerate(DILATIONS):
        pl.when(step == g)(functools.partial(group_step, g, d))


def _attn_inproj(x, sc, sh, w_bf16):
    B, S, D = x.shape
    P = TOKEN_BLOCK
    nblk = S // P
    ncols = 3 * N_GROUPS
    ncw = D // COL_W
    return pl.pallas_call(
        _inproj_kernel,
        grid=(B, nblk, N_GROUPS),
        in_specs=[
            pl.BlockSpec((None, P, D), lambda b, i, g: (b, i, 0)),
            pl.BlockSpec((None, 1, D), lambda b, i, g: (b, 0, 0)),
            pl.BlockSpec((None, 1, D), lambda b, i, g: (b, 0, 0)),
            pl.BlockSpec((D, (ncols + 1) * D), lambda b, i, g: (0, 0),
                         pipeline_mode=pl.Buffered(1)),
        ],
        out_specs=[
            pl.BlockSpec((3, None, None, ncw, P, COL_W), lambda b, i, g: (g, b, i, 0, 0, 0)),
            pl.BlockSpec((None, None, ncw, P, COL_W), lambda b, i, g: (b, i, 0, 0, 0)),
        ],
        out_shape=[jax.ShapeDtypeStruct((ncols, B, nblk, ncw, P, COL_W), BF16),
                   jax.ShapeDtypeStruct((B, nblk, ncw, P, COL_W), BF16)],
        scratch_shapes=[pltpu.VMEM((P, D), BF16),
                        pltpu.VMEM((D // LANES, P, LANES), F32)],
        compiler_params=pltpu.CompilerParams(
            dimension_semantics=("arbitrary", "arbitrary", "arbitrary"),
            vmem_limit_bytes=VMEM_LIMIT),
        name="attn_inproj",
    )(x, sc, sh, w_bf16)


def _attn_kernel(slopes_ref, *refs):
    P = TOKEN_BLOCK
    group_refs = [refs[7 * g:7 * g + 7] for g in range(N_GROUPS)]
    gate_ref = refs[7 * N_GROUPS]
    y_ref = refs[7 * N_GROUPS + 1]
    scr = refs[7 * N_GROUPS + 2:]
    bias_scr, s_scr, p_scr, o_scr, l_scr = scr

    i = pl.program_id(1)
    hq = pl.program_id(2)
    nblk = pl.num_programs(1)

    lane = lax.broadcasted_iota(jnp.int32, (1, COL_W), 1)
    head_masks = [(lane >= h * HEAD_DIM) & (lane < (h + 1) * HEAD_DIM)
                  for h in range(HEADS_PER_STEP)]
    kk = lax.broadcasted_iota(jnp.int32, (K_TILE, COL_W), 0)
    qq = lax.broadcasted_iota(jnp.int32, (K_TILE, COL_W), 1) % Q_CHUNK
    rel = jnp.abs(kk - HALF_WIN - qq)
    dist = rel.astype(F32)
    band = jnp.where(rel <= HALF_WIN, 0.0, NEG_INF).astype(F32)
    lo_pen = jnp.where(kk < HALF_WIN, NEG_INF, 0.0).astype(F32)
    hi_pen = jnp.where(kk >= K_TILE - HALF_WIN, NEG_INF, 0.0).astype(F32)
    pair_lo = lax.broadcasted_iota(jnp.int32, (1, LANES), 1) < HEAD_DIM
    nt_dims = (((1,), (1,)), ((), ()))
    tn_dims = (((0,), (0,)), ((), ()))
    n_units = P // Q_CHUNK
    n_steps = n_units // UNITS_PER_STEP
    assert n_units % UNITS_PER_STEP == 0
    lse_rows = lax.broadcasted_iota(jnp.int32, (P_ROWS - K_TILE, LANES), 0) < 3
    rhs_tail = jnp.concatenate(
        [jnp.zeros((P_ROWS - K_TILE, LANES), BF16), jnp.where(lse_rows, 1.0, 0.0).astype(BF16)],
        axis=1)
    p_scr[:, :, K_TILE:, :] = jnp.zeros((2, UNITS_PER_STEP, P_ROWS - K_TILE, COL_W), BF16)

    stages = []
    for g, d in enumerate(DILATIONS):
        q_ref, k_ref, v_ref, kp_ref, kn_ref, vp_ref, vn_ref = group_refs[g]
        rows = P // d
        nc = rows // Q_CHUNK

        def halo_tile(u, cur, prev, nxt, nc=nc):
            r, c = u // nc, u % nc
            if 0 < c < nc - 1:
                return cur[r, (c - 1) * Q_CHUNK:(c + 2) * Q_CHUNK, :]
            lo = prev[r] if c == 0 else cur[r, (c - 1) * Q_CHUNK:c * Q_CHUNK, :]
            mid = cur[r, c * Q_CHUNK:(c + 1) * Q_CHUNK, :]
            hi = nxt[r] if c == nc - 1 else cur[r, (c + 1) * Q_CHUNK:(c + 2) * Q_CHUNK, :]
            return jnp.concatenate([lo, mid, hi], axis=0)

        slope_lane = jnp.zeros((1, COL_W), F32)
        for h in range(HEADS_PER_STEP):
            slope = slopes_ref[g * HEADS + hq * HEADS_PER_STEP + h] * float(d)
            slope_lane = jnp.where(head_masks[h], slope, slope_lane)
        base = (band - slope_lane * dist) * LOG2E
        bias_scr[g, 0] = base
        bias_scr[g, 1] = base + lo_pen
        bias_scr[g, 2] = base + hi_pen

        def coords(u, d=d, nc=nc):
            r, c = u // nc, u % nc
            off = c * Q_CHUNK
            tok0 = c * (Q_CHUNK * d) + r
            if not isinstance(u, int):
                off = pl.multiple_of(off, Q_CHUNK)
            if d == 1:
                dst = pl.ds(off, Q_CHUNK)
            else:
                dst = pl.ds(tok0, Q_CHUNK, stride=d)
            return r, c, off, dst

        def qk(u, s_buf, g=g, q_ref=q_ref, k_ref=k_ref, kp_ref=kp_ref, kn_ref=kn_ref,
               halo_tile=halo_tile, coords=coords, nc=nc):
            r, _, off, _ = coords(u)
            q = q_ref[r, pl.ds(off, Q_CHUNK), :]
            kt = halo_tile(u, k_ref, kp_ref, kn_ref)
            zero = jnp.zeros_like(q)
            qs = jnp.concatenate([jnp.where(hm, q, zero) for hm in head_masks], axis=0)
            _, c, _, _ = coords(u)
            var = jnp.where((i == 0) & (c == 0), 1,
                            jnp.where((i == nblk - 1) & (c == nc - 1), 2, 0))
            s_buf[...] = (lax.dot_general(kt, qs, nt_dims, preferred_element_type=F32)
                          + bias_scr[g, var])

        def softmax(u, s_buf, p_buf):
            s = s_buf[...]
            m = jnp.max(s, axis=0, keepdims=True)
            p = jnp.exp2(s - m)
            den = jnp.sum(p, axis=0, keepdims=True)
            p_buf[0:K_TILE, :] = (p * (1.0 / den)).astype(BF16)
            lse = m + jnp.log(den) * LOG2E
            hi = lse.astype(BF16)
            rest = lse - hi.astype(F32)
            mid = rest.astype(BF16)
            lo = (rest - mid.astype(F32)).astype(BF16)
            p_buf[K_TILE:K_TILE + BF16_ROWS, :] = jnp.concatenate(
                [hi, mid, lo, jnp.zeros((BF16_ROWS - 3, COL_W), BF16)], axis=0)

        def pv(u, p_buf, g=g, v_ref=v_ref, vp_ref=vp_ref, vn_ref=vn_ref,
               halo_tile=halo_tile, coords=coords):
            r, _, off, dst = coords(u)
            vt = halo_tile(u, v_ref, vp_ref, vn_ref)
            for k in range(HEADS_PER_STEP // 2):
                cs = slice(k * LANES, (k + 1) * LANES)
                rhs = jnp.concatenate(
                    [jnp.concatenate([vt[:, cs], jnp.zeros((K_TILE, LANES), BF16)], axis=1),
                     rhs_tail], axis=0)
                out = lax.dot_general(p_buf[:, cs], rhs, tn_dims, preferred_element_type=F32)
                even, odd = out[0:Q_CHUNK], out[Q_CHUNK:2 * Q_CHUNK]
                o_scr[g, k, dst, :] = jnp.where(pair_lo, even[:, 0:LANES], odd[:, 0:LANES])
                l_scr[g, k, dst, :] = jnp.where(pair_lo, even[:, LANES:], odd[:, LANES:])

        stages.append((qk, softmax, pv))

    assert DILATIONS[0] == 1
    group_order = list(range(1, N_GROUPS)) + [0]
    sets = [(g, st) for g in group_order for st in range(n_steps)]
    mrows = UNITS_PER_STEP * Q_CHUNK

    def merge(t):
        sl = slice(t * mrows, (t + 1) * mrows)
        for cc in range(COL_W // LANES):
            cs = slice(cc * LANES, (cc + 1) * LANES)
            l0, l1, l2 = l_scr[0, cc, sl, :], l_scr[1, cc, sl, :], l_scr[2, cc, sl, :]
            mx = jnp.maximum(jnp.maximum(l0, l1), l2)
            e0, e1, e2 = jnp.exp2(l0 - mx), jnp.exp2(l1 - mx), jnp.exp2(l2 - mx)
            inv = 1.0 / (e0 + e1 + e2)
            o = (e0 * o_scr[0, cc, sl, :] + e1 * o_scr[1, cc, sl, :]
                 + e2 * o_scr[2, cc, sl, :]) * inv
            y_ref[sl, cs] = (o * gate_ref[sl, cs].astype(F32)).astype(BF16)

    for t in range(len(sets) + 2):
        par = t % 2
        for j in range(UNITS_PER_STEP):
            if 0 <= t - 2 < len(sets):
                g, st = sets[t - 2]
                stages[g][2](UNITS_PER_STEP * st + j, p_scr.at[par, j])
            if t < len(sets):
                g, st = sets[t]
                stages[g][0](UNITS_PER_STEP * st + j, s_scr.at[par, j])
        for j in range(UNITS_PER_STEP):
            if 0 <= t - 1 < len(sets):
                g, st = sets[t - 1]
                stages[g][1](UNITS_PER_STEP * st + j, s_scr.at[1 - par, j], p_scr.at[1 - par, j])
        if 0 <= t - 2 < len(sets) and sets[t - 2][0] == 0:
            merge(sets[t - 2][1])


def _attention(z, silu_gate, slopes, B, S, D):
    P = TOKEN_BLOCK
    nblk = S // P
    ncw = D // COL_W
    ncols = z.shape[0]
    in_specs = []
    operands = []
    scratch = []
    for g, d in enumerate(DILATIONS):
        rows = P // d
        nh = rows // HALF_WIN
        zg = z.reshape(ncols, B, nblk, ncw, d, rows, COL_W)
        zh = z.reshape(ncols, B, nblk, ncw, d, nh, HALF_WIN, COL_W)
        full = (None, None, None, None, d, rows, COL_W)
        halo = (None, None, None, None, d, None, HALF_WIN, COL_W)

        def cur(n):
            return pl.BlockSpec(full, lambda b, i, hq, s, n=n: (n, b, i, hq, 0, 0, 0))

        def prev(n, nh=nh):
            return pl.BlockSpec(halo, lambda b, i, hq, s, n=n, nh=nh:
                                (n, b, jnp.maximum(i - 1, 0), hq, 0, nh - 1, 0, 0))

        def nxt(n, nblk=nblk):
            return pl.BlockSpec(halo, lambda b, i, hq, s, n=n, nblk=nblk:
                                (n, b, jnp.minimum(i + 1, nblk - 1), hq, 0, 0, 0, 0))

        nq, nk, nv = 3 * g, 3 * g + 1, 3 * g + 2
        in_specs += [cur(nq), cur(nk), cur(nv), prev(nk), nxt(nk), prev(nv), nxt(nv)]
        operands += [zg, zg, zg, zh, zh, zh, zh]
    in_specs.append(pl.BlockSpec((None, None, None, P, COL_W),
                                 lambda b, i, hq, s: (b, i, hq, 0, 0)))
    operands.append(silu_gate)
    scratch += [
        pltpu.VMEM((N_GROUPS, 3, K_TILE, COL_W), F32),
        pltpu.VMEM((2, UNITS_PER_STEP, K_TILE, COL_W), F32),
        pltpu.VMEM((2, UNITS_PER_STEP, P_ROWS, COL_W), BF16),
        pltpu.VMEM((N_GROUPS, COL_W // LANES, P, LANES), F32),
        pltpu.VMEM((N_GROUPS, COL_W // LANES, P, LANES), F32),
    ]
    grid_spec = pltpu.PrefetchScalarGridSpec(
        num_scalar_prefetch=1,
        grid=(B, nblk, ncw),
        in_specs=in_specs,
        out_specs=pl.BlockSpec((None, P, COL_W), lambda b, i, hq, s: (b, i, hq)),
        scratch_shapes=scratch,
    )
    return pl.pallas_call(
        _attn_kernel,
        grid_spec=grid_spec,
        out_shape=jax.ShapeDtypeStruct((B, S, D), BF16),
        compiler_params=pltpu.CompilerParams(
            dimension_semantics=("arbitrary", "arbitrary", "arbitrary"),
            vmem_limit_bytes=VMEM_LIMIT),
        name="dilated_attn",
    )(slopes, *operands)


SGU_COLS = 256


def _gelu_exact(x):
    return 0.5 * x * (1.0 + lax.erf(x * (2.0 ** -0.5)))


def _sgu_kernel(ya_ref, x_ref, gca_ref, wa_ref, lga_ref, lba_ref,
                sc_ref, sh_ref, gc_ref, win_ref, lng_ref, lnb_ref, ws_ref, bs_ref,
                wout_ref, plg_ref, plb_ref, o_ref, x1_scr, h_scr, v_scr, y_scr):
    tm = x_ref.shape[0]
    E = v_scr.shape[1]

    halves = [slice(hh * (tm // 2), (hh + 1) * (tm // 2)) for hh in range(2)]

    for rs in halves:
        out0 = jnp.dot(ya_ref[rs, :], wa_ref[...], preferred_element_type=F32)
        res0 = DEEPNORM_ALPHA * x_ref[rs, :] + gca_ref[...] * out0
        x1 = _layer_norm_rows(res0, lga_ref[...], lba_ref[...])
        x1_scr[rs, :] = x1
        h_scr[rs, :] = (x1 * sc_ref[...] + sh_ref[...]).astype(BF16)

    for rs in halves:
        zv = jnp.dot(h_scr[rs, :], win_ref[:, E:2 * E], preferred_element_type=F32)
        v = _layer_norm_rows(_gelu_exact(zv), lng_ref[...], lnb_ref[...])
        v_scr[rs, :] = v.astype(BF16)

    gpb = SGU_COLS // SGU_GROUP_CH
    for cb in range(E // SGU_COLS):
        c0 = cb * SGU_COLS
        zu = jnp.dot(h_scr[...], win_ref[:, c0:c0 + SGU_COLS], preferred_element_type=F32)
        zg = jnp.dot(h_scr[...], win_ref[:, 2 * E + c0:2 * E + c0 + SGU_COLS],
                     preferred_element_type=F32)
        ug = _gelu_exact(zu) * _silu(zg)
        for gi in range(gpb):
            g = cb * gpb + gi
            l0 = gi * SGU_GROUP_CH
            wsg = ws_ref[g]
            bsg = bs_ref[:, g:g + 1]
            gcols = slice(c0 + l0, c0 + l0 + SGU_GROUP_CH)
            for nn in range(0, tm // SGU_CHUNK, 2):
                ra = slice(nn * SGU_CHUNK, (nn + 1) * SGU_CHUNK)
                rb = slice((nn + 1) * SGU_CHUNK, (nn + 2) * SGU_CHUNK)
                vc = jnp.concatenate([v_scr[ra, gcols], v_scr[rb, gcols]], axis=1)
                sv = jnp.dot(wsg, vc, preferred_element_type=F32) + bsg
                for rr, half in ((ra, 0), (rb, 1)):
                    y = (ug[rr, l0:l0 + SGU_GROUP_CH]
                         * sv[:, half * SGU_GROUP_CH:(half + 1) * SGU_GROUP_CH])
                    y_scr[rr, gcols] = y.astype(BF16)

    for rs in halves:
        out = jnp.dot(y_scr[rs, :], wout_ref[...], preferred_element_type=F32)
        res = DEEPNORM_ALPHA * x1_scr[rs, :] + gc_ref[...] * out
        o_ref[rs, :] = _layer_norm_rows(res, plg_ref[...], plb_ref[...])


def _outproj_sgu_layer(y_a, x, gate_c_a, w_out_a, post_g_a, post_b_a,
                       sc, sh, gate_c, w_in, ln_g, ln_b, w_s, b_s_t, w_out, post_g, post_b):
    B, S, D = x.shape
    E = w_out.shape[0]
    G = w_s.shape[0]
    tm = 512
    const2 = lambda b, i: (0, 0)
    single = pl.Buffered(1)
    return pl.pallas_call(
        _sgu_kernel,
        grid=(B, S // tm),
        in_specs=[
            pl.BlockSpec((None, tm, D), lambda b, i: (b, i, 0)),
            pl.BlockSpec((None, tm, D), lambda b, i: (b, i, 0)),
            pl.BlockSpec((None, 1, D), lambda b, i: (b, 0, 0)),
            pl.BlockSpec((D, D), const2, pipeline_mode=single),
            pl.BlockSpec((1, D), const2),
            pl.BlockSpec((1, D), const2),
            pl.BlockSpec((None, 1, D), lambda b, i: (b, 0, 0)),
            pl.BlockSpec((None, 1, D), lambda b, i: (b, 0, 0)),
            pl.BlockSpec((None, 1, D), lambda b, i: (b, 0, 0)),
            pl.BlockSpec((D, 3 * E), const2, pipeline_mode=single),
            pl.BlockSpec((1, E), const2),
            pl.BlockSpec((1, E), const2),
            pl.BlockSpec((G, SGU_CHUNK, SGU_CHUNK), lambda b, i: (0, 0, 0), pipeline_mode=single),
            pl.BlockSpec((SGU_CHUNK, G), const2),
            pl.BlockSpec((E, D), const2, pipeline_mode=single),
            pl.BlockSpec((1, D), const2),
            pl.BlockSpec((1, D), const2),
        ],
        out_specs=pl.BlockSpec((None, tm, D), lambda b, i: (b, i, 0)),
        out_shape=jax.ShapeDtypeStruct((B, S, D), F32),
        scratch_shapes=[
            pltpu.VMEM((tm, D), F32),
            pltpu.VMEM((tm, D), BF16),
            pltpu.VMEM((tm, E), BF16),
            pltpu.VMEM((tm, E), BF16),
        ],
        compiler_params=pltpu.CompilerParams(
            dimension_semantics=("arbitrary", "arbitrary"),
            vmem_limit_bytes=VMEM_LIMIT),
        name="outproj_sgu_layer",
    )(y_a, x, gate_c_a, w_out_a, post_g_a, post_b_a,
      sc, sh, gate_c, w_in, ln_g, ln_b, w_s, b_s_t, w_out, post_g, post_b)


def _alibi_slopes(n):
    return 2.0 ** (-8.0 * jnp.arange(1, n + 1, dtype=F32) / n)


def kernel(x, c, ada_w, ada_b, post_ln_g, post_ln_b, a_w_in, a_w_out, b_w_in, b_ln_g, b_ln_b,
           b_w_s, b_b_s, b_w_out):
    B, S, D = x.shape
    mod = _modulation(c, ada_w, ada_b)
    slopes = _alibi_slopes(N_GROUPS * HEADS)

    def mod_parts(i):
        shift = mod[i, :, None, 0:D]
        scale1 = 1.0 + mod[i, :, None, D:2 * D]
        gate_c = mod[i, :, None, 2 * D:3 * D]
        return scale1, shift, gate_c

    sc, sh, gc_a = mod_parts(0)
    z, silu_gate = _attn_inproj(x, sc, sh, a_w_in[0].astype(BF16))
    y = _attention(z, silu_gate, slopes, B, S, D)

    sc, sh, gc = mod_parts(1)
    return _outproj_sgu_layer(
        y, x, gc_a, a_w_out[0].astype(BF16), post_ln_g[0][None, :], post_ln_b[0][None, :],
        sc, sh, gc, b_w_in[0].astype(BF16), b_ln_g[0][None, :], b_ln_b[0][None, :],
        b_w_s[0].astype(BF16), b_b_s[0].T, b_w_out[0].astype(BF16),
        post_ln_g[1][None, :], post_ln_b[1][None, :])
```

```python
import functools

import jax
import jax.numpy as jnp
from jax import lax
from jax.experimental import pallas as pl
from jax.experimental.pallas import tpu as pltpu

F32 = jnp.float32
BF16 = jnp.bfloat16

DEPTH = 2
ATTN_GROUPS = ((128, 1), (512, 4), (2048, 16))
N_GROUPS = len(ATTN_GROUPS)
DILATIONS = tuple(d for _, d in ATTN_GROUPS)
HEADS = 16
HEAD_DIM = 64
HALF_WIN = 64
SGU_CHUNK = 128
SGU_GROUP_CH = 128
DEEPNORM_ALPHA = (2 * DEPTH) ** 0.25
LN_EPS = 1e-5
NEG_INF = -1e30
LOG2E = 1.4426950408889634
LANES = 128

TOKEN_BLOCK = 1024
HEADS_PER_STEP = 4
COL_W = HEADS_PER_STEP * HEAD_DIM
Q_CHUNK = 64
K_TILE = Q_CHUNK + 2 * HALF_WIN
UNITS_PER_STEP = 2
BF16_ROWS = 16
P_ROWS = 256
V7X_VMEM_BYTES = 64 * 1024 * 1024
VMEM_LIMIT = V7X_VMEM_BYTES - 8 * 1024 * 1024

assert all(w // (2 * d) == HALF_WIN for w, d in ATTN_GROUPS)


def _silu(x):
    return x * jax.nn.sigmoid(x)


def _layer_norm_rows(x, g, b):
    mu = jnp.mean(x, axis=-1, keepdims=True)
    xc = x - mu
    var = jnp.mean(xc * xc, axis=-1, keepdims=True)
    return xc * lax.rsqrt(var + LN_EPS) * g + b


def _mod_kernel(ct_ref, w_ref, b_ref, o_ref):
    ct = ct_ref[...]
    cond = _silu(ct)
    w = w_ref[...]
    for b in range(ct.shape[1]):
        col = cond[:, b:b + 1]
        o_ref[b:b + 1, :] = jnp.sum(col * w, axis=0, keepdims=True) + b_ref[...]


def _modulation(c, ada_w, ada_b):
    B, D = c.shape
    depth, _, n3 = ada_w.shape
    tn = 1024
    return pl.pallas_call(
        _mod_kernel,
        grid=(depth, n3 // tn),
        in_specs=[
            pl.BlockSpec((D, B), lambda i, n: (0, 0)),
            pl.BlockSpec((None, D, tn), lambda i, n: (i, 0, n)),
            pl.BlockSpec((None, 1, tn), lambda i, n: (i, 0, n)),
        ],
        out_specs=pl.BlockSpec((None, B, tn), lambda i, n: (i, 0, n)),
        out_shape=jax.ShapeDtypeStruct((depth, B, n3), F32),
        compiler_params=pltpu.CompilerParams(
            dimension_semantics=("arbitrary", "arbitrary")),
        name="adaln_mod",
    )(c.T, ada_w, ada_b.reshape(depth, 1, n3))


def _inproj_kernel(x_ref, sc_ref, sh_ref, w_ref, z_ref, sg_ref, h_scr, xs_scr):
    step = pl.program_id(2)
    tm, D = x_ref.shape

    def group_step(g, d):
        rows = tm // d
        for cc in range(D // LANES):
            cs = slice(cc * LANES, (cc + 1) * LANES)
            if g == 0:
                h = x_ref[:, cs] * sc_ref[:, cs] + sh_ref[:, cs]
                xs_scr[cc] = h
                h_scr[:, cs] = h.astype(BF16)
            else:
                for r in range(d):
                    hr = xs_scr[cc, pl.ds(r, rows, stride=d), :]
                    h_scr[r * rows:(r + 1) * rows, cs] = hr.astype(BF16)
        for j in range(3):
            n = 3 * g + j
            z = jnp.dot(h_scr[...], w_ref[:, n * D:(n + 1) * D], preferred_element_type=F32)
            if j == 0:
                z = z * (HEAD_DIM ** -0.5 * LOG2E)
            z = z.astype(BF16)
            for cc in range(z_ref.shape[1]):
                z_ref[j, cc] = z[:, cc * COL_W:(cc + 1) * COL_W]
        if g == 0:
            n = 3 * N_GROUPS
            sg = _silu(jnp.dot(h_scr[...], w_ref[:, n * D:(n + 1) * D],
                               preferred_element_type=F32)).astype(BF16)
            for cc in range(sg_ref.shape[0]):
                sg_ref[cc] = sg[:, cc * COL_W:(cc + 1) * COL_W]

    for g, d in enumerate(DILATIONS):
        pl.when(step == g)(functools.partial(group_step, g, d))


def _attn_inproj(x, sc, sh, w_bf16):
    B, S, D = x.shape
    P = TOKEN_BLOCK
    nblk = S // P
    ncols = 3 * N_GROUPS
    ncw = D // COL_W
    return pl.pallas_call(
        _inproj_kernel,
        grid=(B, nblk, N_GROUPS),
        in_specs=[
            pl.BlockSpec((None, P, D), lambda b, i, g: (b, i, 0)),
            pl.BlockSpec((None, 1, D), lambda b, i, g: (b, 0, 0)),
            pl.BlockSpec((None, 1, D), lambda b, i, g: (b, 0, 0)),
            pl.BlockSpec((D, (ncols + 1) * D), lambda b, i, g: (0, 0),
                         pipeline_mode=pl.Buffered(1)),
        ],
        out_specs=[
            pl.BlockSpec((3, None, None, ncw, P, COL_W), lambda b, i, g: (g, b, i, 0, 0, 0)),
            pl.BlockSpec((None, None, ncw, P, COL_W), lambda b, i, g: (b, i, 0, 0, 0)),
        ],
        out_shape=[jax.ShapeDtypeStruct((ncols, B, nblk, ncw, P, COL_W), BF16),
                   jax.ShapeDtypeStruct((B, nblk, ncw, P, COL_W), BF16)],
        scratch_shapes=[pltpu.VMEM((P, D), BF16),
                        pltpu.VMEM((D // LANES, P, LANES), F32)],
        compiler_params=pltpu.CompilerParams(
            dimension_semantics=("arbitrary", "arbitrary", "arbitrary"),
            vmem_limit_bytes=VMEM_LIMIT),
        name="attn_inproj",
    )(x, sc, sh, w_bf16)


def _attn_kernel(slopes_ref, *refs):
    P = TOKEN_BLOCK
    group_refs = [refs[7 * g:7 * g + 7] for g in range(N_GROUPS)]
    gate_ref = refs[7 * N_GROUPS]
    y_ref = refs[7 * N_GROUPS + 1]
    scr = refs[7 * N_GROUPS + 2:]
    bias_scr, s_scr, p_scr, o_scr, l_scr = scr

    i = pl.program_id(1)
    hq = pl.program_id(2)
    nblk = pl.num_programs(1)

    lane = lax.broadcasted_iota(jnp.int32, (1, COL_W), 1)
    head_masks = [(lane >= h * HEAD_DIM) & (lane < (h + 1) * HEAD_DIM)
                  for h in range(HEADS_PER_STEP)]
    kk = lax.broadcasted_iota(jnp.int32, (K_TILE, COL_W), 0)
    qq = lax.broadcasted_iota(jnp.int32, (K_TILE, COL_W), 1) % Q_CHUNK
    rel = jnp.abs(kk - HALF_WIN - qq)
    dist = rel.astype(F32)
    band = jnp.where(rel <= HALF_WIN, 0.0, NEG_INF).astype(F32)
    lo_pen = jnp.where(kk < HALF_WIN, NEG_INF, 0.0).astype(F32)
    hi_pen = jnp.where(kk >= K_TILE - HALF_WIN, NEG_INF, 0.0).astype(F32)
    pair_lo = lax.broadcasted_iota(jnp.int32, (1, LANES), 1) < HEAD_DIM
    nt_dims = (((1,), (1,)), ((), ()))
    tn_dims = (((0,), (0,)), ((), ()))
    n_units = P // Q_CHUNK
    n_steps = n_units // UNITS_PER_STEP
    assert n_units % UNITS_PER_STEP == 0
    lse_rows = lax.broadcasted_iota(jnp.int32, (P_ROWS - K_TILE, LANES), 0) < 3
    rhs_tail = jnp.concatenate(
        [jnp.zeros((P_ROWS - K_TILE, LANES), BF16), jnp.where(lse_rows, 1.0, 0.0).astype(BF16)],
        axis=1)
    p_scr[:, :, K_TILE:, :] = jnp.zeros((2, UNITS_PER_STEP, P_ROWS - K_TILE, COL_W), BF16)

    stages = []
    for g, d in enumerate(DILATIONS):
        q_ref, k_ref, v_ref, kp_ref, kn_ref, vp_ref, vn_ref = group_refs[g]
        rows = P // d
        nc = rows // Q_CHUNK

        def halo_tile(u, cur, prev, nxt, nc=nc):
            r, c = u // nc, u % nc
            if 0 < c < nc - 1:
                return cur[r, (c - 1) * Q_CHUNK:(c + 2) * Q_CHUNK, :]
            lo = prev[r] if c == 0 else cur[r, (c - 1) * Q_CHUNK:c * Q_CHUNK, :]
            mid = cur[r, c * Q_CHUNK:(c + 1) * Q_CHUNK, :]
            hi = nxt[r] if c == nc - 1 else cur[r, (c + 1) * Q_CHUNK:(c + 2) * Q_CHUNK, :]
            return jnp.concatenate([lo, mid, hi], axis=0)

        slope_lane = jnp.zeros((1, COL_W), F32)
        for h in range(HEADS_PER_STEP):
            slope = slopes_ref[g * HEADS + hq * HEADS_PER_STEP + h] * float(d)
            slope_lane = jnp.where(head_masks[h], slope, slope_lane)
        base = (band - slope_lane * dist) * LOG2E
        bias_scr[g, 0] = base
        bias_scr[g, 1] = base + lo_pen
        bias_scr[g, 2] = base + hi_pen

        def coords(u, d=d, nc=nc):
            r, c = u // nc, u % nc
            off = c * Q_CHUNK
            tok0 = c * (Q_CHUNK * d) + r
            if d == 1:
                dst = pl.ds(off, Q_CHUNK)
            else:
                dst = pl.ds(tok0, Q_CHUNK, stride=d)
            return r, c, off, dst

        def qk(u, s_buf, g=g, q_ref=q_ref, k_ref=k_ref, kp_ref=kp_ref, kn_ref=kn_ref,
               halo_tile=halo_tile, coords=coords, nc=nc):
            r, _, off, _ = coords(u)
            q = q_ref[r, pl.ds(off, Q_CHUNK), :]
            kt = halo_tile(u, k_ref, kp_ref, kn_ref)
            zero = jnp.zeros_like(q)
            qs = jnp.concatenate([jnp.where(hm, q, zero) for hm in head_masks], axis=0)
            _, c, _, _ = coords(u)
            var = jnp.where((i == 0) & (c == 0), 1,
                            jnp.where((i == nblk - 1) & (c == nc - 1), 2, 0))
            s_buf[...] = (lax.dot_general(kt, qs, nt_dims, preferred_element_type=F32)
                          + bias_scr[g, var])

        def softmax(u, s_buf, p_buf):
            s = s_buf[...]
            m = jnp.max(s, axis=0, keepdims=True)
            p = jnp.exp2(s - m)
            den = jnp.sum(p, axis=0, keepdims=True)
            p_buf[0:K_TILE, :] = (p * (1.0 / den)).astype(BF16)
            lse = m + jnp.log(den) * LOG2E
            hi = lse.astype(BF16)
            rest = lse - hi.astype(F32)
            mid = rest.astype(BF16)
            lo = (rest - mid.astype(F32)).astype(BF16)
            p_buf[K_TILE:K_TILE + BF16_ROWS, :] = jnp.concatenate(
                [hi, mid, lo, jnp.zeros((BF16_ROWS - 3, COL_W), BF16)], axis=0)

        def pv(u, p_buf, g=g, v_ref=v_ref, vp_ref=vp_ref, vn_ref=vn_ref,
               halo_tile=halo_tile, coords=coords):
            r, _, off, dst = coords(u)
            vt = halo_tile(u, v_ref, vp_ref, vn_ref)
            for k in range(HEADS_PER_STEP // 2):
                cs = slice(k * LANES, (k + 1) * LANES)
                rhs = jnp.concatenate(
                    [jnp.concatenate([vt[:, cs], jnp.zeros((K_TILE, LANES), BF16)], axis=1),
                     rhs_tail], axis=0)
                out = lax.dot_general(p_buf[:, cs], rhs, tn_dims, preferred_element_type=F32)
                even, odd = out[0:Q_CHUNK], out[Q_CHUNK:2 * Q_CHUNK]
                o_scr[g, k, dst, :] = jnp.where(pair_lo, even[:, 0:LANES], odd[:, 0:LANES])
                l_scr[g, k, dst, :] = jnp.where(pair_lo, even[:, LANES:], odd[:, LANES:])

        stages.append((qk, softmax, pv))

    assert DILATIONS[0] == 1
    group_order = list(range(1, N_GROUPS)) + [0]
    sets = [(g, st) for g in group_order for st in range(n_steps)]
    mrows = UNITS_PER_STEP * Q_CHUNK

    def merge(t):
        sl = slice(t * mrows, (t + 1) * mrows)
        for cc in range(COL_W // LANES):
            cs = slice(cc * LANES, (cc + 1) * LANES)
            l0, l1, l2 = l_scr[0, cc, sl, :], l_scr[1, cc, sl, :], l_scr[2, cc, sl, :]
            mx = jnp.maximum(jnp.maximum(l0, l1), l2)
            e0, e1, e2 = jnp.exp2(l0 - mx), jnp.exp2(l1 - mx), jnp.exp2(l2 - mx)
            inv = 1.0 / (e0 + e1 + e2)
            o = (e0 * o_scr[0, cc, sl, :] + e1 * o_scr[1, cc, sl, :]
                 + e2 * o_scr[2, cc, sl, :]) * inv
            y_ref[sl, cs] = (o * gate_ref[sl, cs].astype(F32)).astype(BF16)

    for t in range(len(sets) + 2):
        par = t % 2
        for j in range(UNITS_PER_STEP):
            if 0 <= t - 2 < len(sets):
                g, st = sets[t - 2]
                stages[g][2](UNITS_PER_STEP * st + j, p_scr.at[par, j])
            if t < len(sets):
                g, st = sets[t]
                stages[g][0](UNITS_PER_STEP * st + j, s_scr.at[par, j])
        for j in range(UNITS_PER_STEP):
            if 0 <= t - 1 < len(sets):
                g, st = sets[t - 1]
                stages[g][1](UNITS_PER_STEP * st + j, s_scr.at[1 - par, j], p_scr.at[1 - par, j])
        if 0 <= t - 2 < len(sets) and sets[t - 2][0] == 0:
            merge(sets[t - 2][1])


def _attention(z, silu_gate, slopes, B, S, D):
    P = TOKEN_BLOCK
    nblk = S // P
    ncw = D // COL_W
    ncols = z.shape[0]
    in_specs = []
    operands = []
    scratch = []
    for g, d in enumerate(DILATIONS):
        rows = P // d
        nh = rows // HALF_WIN
        zg = z.reshape(ncols, B, nblk, ncw, d, rows, COL_W)
        zh = z.reshape(ncols, B, nblk, ncw, d, nh, HALF_WIN, COL_W)
        full = (None, None, None, None, d, rows, COL_W)
        halo = (None, None, None, None, d, None, HALF_WIN, COL_W)

        def cur(n):
            return pl.BlockSpec(full, lambda b, i, hq, s, n=n: (n, b, i, hq, 0, 0, 0))

        def prev(n, nh=nh):
            return pl.BlockSpec(halo, lambda b, i, hq, s, n=n, nh=nh:
                                (n, b, jnp.maximum(i - 1, 0), hq, 0, nh - 1, 0, 0))

        def nxt(n, nblk=nblk):
            return pl.BlockSpec(halo, lambda b, i, hq, s, n=n, nblk=nblk:
                                (n, b, jnp.minimum(i + 1, nblk - 1), hq, 0, 0, 0, 0))

        nq, nk, nv = 3 * g, 3 * g + 1, 3 * g + 2
        in_specs += [cur(nq), cur(nk), cur(nv), prev(nk), nxt(nk), prev(nv), nxt(nv)]
        operands += [zg, zg, zg, zh, zh, zh, zh]
    in_specs.append(pl.BlockSpec((None, None, None, P, COL_W),
                                 lambda b, i, hq, s: (b, i, hq, 0, 0)))
    operands.append(silu_gate)
    scratch += [
        pltpu.VMEM((N_GROUPS, 3, K_TILE, COL_W), F32),
        pltpu.VMEM((2, UNITS_PER_STEP, K_TILE, COL_W), F32),
        pltpu.VMEM((2, UNITS_PER_STEP, P_ROWS, COL_W), BF16),
        pltpu.VMEM((N_GROUPS, COL_W // LANES, P, LANES), F32),
        pltpu.VMEM((N_GROUPS, COL_W // LANES, P, LANES), F32),
    ]
    grid_spec = pltpu.PrefetchScalarGridSpec(
        num_scalar_prefetch=1,
        grid=(B, nblk, ncw),
        in_specs=in_specs,
        out_specs=pl.BlockSpec((None, P, COL_W), lambda b, i, hq, s: (b, i, hq)),
        scratch_shapes=scratch,
    )
    return pl.pallas_call(
        _attn_kernel,
        grid_spec=grid_spec,
        out_shape=jax.ShapeDtypeStruct((B, S, D), BF16),
        compiler_params=pltpu.CompilerParams(
            dimension_semantics=("arbitrary", "arbitrary", "arbitrary"),
            vmem_limit_bytes=VMEM_LIMIT),
        name="dilated_attn",
    )(slopes, *operands)


SGU_COLS = 256


def _gelu_exact(x):
    return 0.5 * x * (1.0 + lax.erf(x * (2.0 ** -0.5)))


def _sgu_kernel(ya_ref, x_ref, gca_ref, wa_ref, lga_ref, lba_ref,
                sc_ref, sh_ref, gc_ref, win_ref, lng_ref, lnb_ref, ws_ref, bs_ref,
                wout_ref, plg_ref, plb_ref, o_ref, x1_scr, h_scr, v_scr, y_scr):
    tm = x_ref.shape[0]
    E = v_scr.shape[1]

    halves = [slice(hh * (tm // 2), (hh + 1) * (tm // 2)) for hh in range(2)]

    for rs in halves:
        out0 = jnp.dot(ya_ref[rs, :], wa_ref[...], preferred_element_type=F32)
        res0 = DEEPNORM_ALPHA * x_ref[rs, :] + gca_ref[...] * out0
        x1 = _layer_norm_rows(res0, lga_ref[...], lba_ref[...])
        x1_scr[rs, :] = x1
        h_scr[rs, :] = (x1 * sc_ref[...] + sh_ref[...]).astype(BF16)

    for rs in halves:
        zv = jnp.dot(h_scr[rs, :], win_ref[:, E:2 * E], preferred_element_type=F32)
        v = _layer_norm_rows(_gelu_exact(zv), lng_ref[...], lnb_ref[...])
        v_scr[rs, :] = v.astype(BF16)

    gpb = SGU_COLS // SGU_GROUP_CH
    for cb in range(E // SGU_COLS):
        c0 = cb * SGU_COLS
        zu = jnp.dot(h_scr[...], win_ref[:, c0:c0 + SGU_COLS], preferred_element_type=F32)
        zg = jnp.dot(h_scr[...], win_ref[:, 2 * E + c0:2 * E + c0 + SGU_COLS],
                     preferred_element_type=F32)
        ug = _gelu_exact(zu) * _silu(zg)
        for gi in range(gpb):
            g = cb * gpb + gi
            l0 = gi * SGU_GROUP_CH
            wsg = ws_ref[g]
            bsg = bs_ref[:, g:g + 1]
            gcols = slice(c0 + l0, c0 + l0 + SGU_GROUP_CH)
            for nn in range(0, tm // SGU_CHUNK, 2):
                ra = slice(nn * SGU_CHUNK, (nn + 1) * SGU_CHUNK)
                rb = slice((nn + 1) * SGU_CHUNK, (nn + 2) * SGU_CHUNK)
                vc = jnp.concatenate([v_scr[ra, gcols], v_scr[rb, gcols]], axis=1)
                sv = jnp.dot(wsg, vc, preferred_element_type=F32) + bsg
                for rr, half in ((ra, 0), (rb, 1)):
                    y = (ug[rr, l0:l0 + SGU_GROUP_CH]
                         * sv[:, half * SGU_GROUP_CH:(half + 1) * SGU_GROUP_CH])
                    y_scr[rr, gcols] = y.astype(BF16)

    for rs in halves:
        out = jnp.dot(y_scr[rs, :], wout_ref[...], preferred_element_type=F32)
        res = DEEPNORM_ALPHA * x1_scr[rs, :] + gc_ref[...] * out
        o_ref[rs, :] = _layer_norm_rows(res, plg_ref[...], plb_ref[...])


def _outproj_sgu_layer(y_a, x, gate_c_a, w_out_a, post_g_a, post_b_a,
                       sc, sh, gate_c, w_in, ln_g, ln_b, w_s, b_s_t, w_out, post_g, post_b):
    B, S, D = x.shape
    E = w_out.shape[0]
    G = w_s.shape[0]
    tm = 512
    const2 = lambda b, i: (0, 0)
    single = pl.Buffered(1)
    return pl.pallas_call(
        _sgu_kernel,
        grid=(B, S // tm),
        in_specs=[
            pl.BlockSpec((None, tm, D), lambda b, i: (b, i, 0)),
            pl.BlockSpec((None, tm, D), lambda b, i: (b, i, 0)),
            pl.BlockSpec((None, 1, D), lambda b, i: (b, 0, 0)),
            pl.BlockSpec((D, D), const2, pipeline_mode=single),
            pl.BlockSpec((1, D), const2),
            pl.BlockSpec((1, D), const2),
            pl.BlockSpec((None, 1, D), lambda b, i: (b, 0, 0)),
            pl.BlockSpec((None, 1, D), lambda b, i: (b, 0, 0)),
            pl.BlockSpec((None, 1, D), lambda b, i: (b, 0, 0)),
            pl.BlockSpec((D, 3 * E), const2, pipeline_mode=single),
            pl.BlockSpec((1, E), const2),
            pl.BlockSpec((1, E), const2),
            pl.BlockSpec((G, SGU_CHUNK, SGU_CHUNK), lambda b, i: (0, 0, 0), pipeline_mode=single),
            pl.BlockSpec((SGU_CHUNK, G), const2),
            pl.BlockSpec((E, D), const2, pipeline_mode=single),
            pl.BlockSpec((1, D), const2),
            pl.BlockSpec((1, D), const2),
        ],
        out_specs=pl.BlockSpec((None, tm, D), lambda b, i: (b, i, 0)),
        out_shape=jax.ShapeDtypeStruct((B, S, D), F32),
        scratch_shapes=[
            pltpu.VMEM((tm, D), F32),
            pltpu.VMEM((tm, D), BF16),
            pltpu.VMEM((tm, E), BF16),
            pltpu.VMEM((tm, E), BF16),
        ],
        compiler_params=pltpu.CompilerParams(
            dimension_semantics=("arbitrary", "arbitrary"),
            vmem_limit_bytes=VMEM_LIMIT),
        name="outproj_sgu_layer",
    )(y_a, x, gate_c_a, w_out_a, post_g_a, post_b_a,
      sc, sh, gate_c, w_in, ln_g, ln_b, w_s, b_s_t, w_out, post_g, post_b)


def _alibi_slopes(n):
    return 2.0 ** (-8.0 * jnp.arange(1, n + 1, dtype=F32) / n)


def kernel(x, c, ada_w, ada_b, post_ln_g, post_ln_b, a_w_in, a_w_out, b_w_in, b_ln_g, b_ln_b,
           b_w_s, b_b_s, b_w_out):
    B, S, D = x.shape
    mod = _modulation(c, ada_w, ada_b)
    slopes = _alibi_slopes(N_GROUPS * HEADS)

    def mod_parts(i):
        shift = mod[i, :, None, 0:D]
        scale1 = 1.0 + mod[i, :, None, D:2 * D]
        gate_c = mod[i, :, None, 2 * D:3 * D]
        return scale1, shift, gate_c

    sc, sh, gc_a = mod_parts(0)
    z, silu_gate = _attn_inproj(x, sc, sh, a_w_in[0].astype(BF16))
    y = _attention(z, silu_gate, slopes, B, S, D)

    sc, sh, gc = mod_parts(1)
    return _outproj_sgu_layer(
        y, x, gc_a, a_w_out[0].astype(BF16), post_ln_g[0][None, :], post_ln_b[0][None, :],
        sc, sh, gc, b_w_in[0].astype(BF16), b_ln_g[0][None, :], b_ln_b[0][None, :],
        b_w_s[0].astype(BF16), b_b_s[0].T, b_w_out[0].astype(BF16),
        post_ln_g[1][None, :], post_ln_b[1][None, :])
```
